```python
import functools
import jax, jax.numpy as jnp
from jax import lax
import numpy as np

D_MODEL = 1024
BATCH = 2
SEQ = 16384
DEPTH = 1
DEC_BATCH = 32
DEC_SEQ = 16
PAST_LEN = 2048

CHUNK = 64
WINDOW = 128
N_WIN_CHUNKS = WINDOW // CHUNK
ATTN_WIDTH = D_MODEL // 2
N_HEADS = 8
HEAD_DIM = ATTN_WIDTH // N_HEADS
N_KV_HEADS = 2
GQA_GROUP = N_HEADS // N_KV_HEADS
Q_DIM = N_HEADS * HEAD_DIM
KV_DIM = N_KV_HEADS * HEAD_DIM
POOL_WIDTH = D_MODEL - ATTN_WIDTH
POOL_WINDOWS = (2, 4, 8, 16)
N_POOL_GROUPS = len(POOL_WINDOWS)
POOL_GROUP_DIM = POOL_WIDTH // N_POOL_GROUPS
POOL_HIST = max(POOL_WINDOWS) - 1
IN_PROJ_DIM = Q_DIM + 2 * KV_DIM + POOL_WIDTH
D_FF = 4 * D_MODEL
PLE_DIM = 256
RMS_EPS = 1e-6

kernel_name = "hymba_swa_sink_pool_stream_step"


def rmsnorm(x, g):
    xf = x.astype(jnp.float32)
    y = xf * lax.rsqrt(jnp.mean(xf * xf, axis=-1, keepdims=True) + RMS_EPS)
    return (y * g.astype(jnp.float32)).astype(x.dtype)


def sink_softmax(s, valid, sink_b):
    s = jnp.where(valid, s, -jnp.inf)
    m = jnp.maximum(jnp.max(s, axis=-1, keepdims=True), sink_b)
    e = jnp.exp(s - m)
    denom = jnp.sum(e, axis=-1, keepdims=True) + jnp.exp(sink_b - m)
    return e / denom


def attn_prompt(q, k, v, sinks):
    B, S = q.shape[:2]
    nc = S // CHUNK
    kl = WINDOW + CHUNK
    pad = ((0, 0), (WINDOW, 0), (0, 0), (0, 0))
    kp = jnp.pad(k, pad).reshape(B, nc + N_WIN_CHUNKS, CHUNK, N_KV_HEADS, HEAD_DIM)
    vp = jnp.pad(v, pad).reshape(B, nc + N_WIN_CHUNKS, CHUNK, N_KV_HEADS, HEAD_DIM)
    kb = jnp.concatenate([kp[:, j:j + nc] for j in range(N_WIN_CHUNKS + 1)], axis=2)
    vb = jnp.concatenate([vp[:, j:j + nc] for j in range(N_WIN_CHUNKS + 1)], axis=2)
    qb = q.reshape(B, nc, CHUNK, N_KV_HEADS, GQA_GROUP, HEAD_DIM)
    s = jnp.einsum('bcqkgd,bcskd->bckgqs', qb, kb,
                   preferred_element_type=jnp.float32) * (HEAD_DIM ** -0.5)
    key_pos = (jnp.arange(nc)[:, None] - N_WIN_CHUNKS) * CHUNK + jnp.arange(kl)[None, :]
    valid = (key_pos >= 0)[None, :, None, None, None, :]
    sink_b = sinks.astype(jnp.float32).reshape(1, 1, N_KV_HEADS, GQA_GROUP, 1, 1)
    p = sink_softmax(s, valid, sink_b)
    o = jnp.einsum('bckgqs,bcskd->bcqkgd', p.astype(v.dtype), vb)
    return o.reshape(B, S, ATTN_WIDTH)


def attn_sample(q, k, v, cache_k, cache_v, sinks):
    B, T = q.shape[:2]
    kf = jnp.concatenate([cache_k.astype(k.dtype), k], axis=1)
    vf = jnp.concatenate([cache_v.astype(v.dtype), v], axis=1)
    qs = q.reshape(B, T, N_KV_HEADS, GQA_GROUP, HEAD_DIM)
    s = jnp.einsum('btkgd,bskd->bkgts', qs, kf,
                   preferred_element_type=jnp.float32) * (HEAD_DIM ** -0.5)
    sink_b = sinks.astype(jnp.float32).reshape(1, N_KV_HEADS, GQA_GROUP, 1, 1)
    p = sink_softmax(s, True, sink_b)
    o = jnp.einsum('bkgts,bskd->btkgd', p.astype(v.dtype), vf)
    return o.reshape(B, T, ATTN_WIDTH)


def multiscale_pool(u, hist, pos0, w_pool, pool_scale):
    T = u.shape[1]
    ext = jnp.concatenate([hist.astype(u.dtype), u], axis=1)
    extf = ext.astype(jnp.float32)
    cs = jnp.cumsum(jnp.pad(extf, ((0, 0), (1, 0), (0, 0))), axis=1)
    pos = pos0 + jnp.arange(T)
    outs = []
    for g, w in enumerate(POOL_WINDOWS):
        lo_c, hi_c = g * POOL_GROUP_DIM, (g + 1) * POOL_GROUP_DIM
        hi = cs[:, POOL_HIST + 1:POOL_HIST + 1 + T, lo_c:hi_c]
        lo = cs[:, POOL_HIST + 1 - w:POOL_HIST + 1 - w + T, lo_c:hi_c]
        cnt = jnp.minimum(pos + 1, w).astype(jnp.float32)[None, :, None]
        d = (hi - lo) / cnt - extf[:, POOL_HIST:, lo_c:hi_c]
        outs.append(jnp.einsum('btc,ce->bte', d, w_pool[g].astype(jnp.float32)))
    b = jnp.concatenate(outs, axis=-1) * pool_scale.astype(jnp.float32)
    return b.astype(u.dtype), ext[:, -POOL_HIST:]


def trunk_layer(x, ple, attn_fn, pool_hist, pos0, ln_mix, w_in, g_attn_out, w_pool,
                pool_scale, g_pool_out, w_out, ln_ffn, w_up, w_down, ln_ple,
                w_ple_gate, w_ple_proj):
    B, T, _ = x.shape
    h = rmsnorm(x, ln_mix)
    z = jnp.einsum('btd,de->bte', h, w_in)
    q = z[..., :Q_DIM].reshape(B, T, N_HEADS, HEAD_DIM)
    k = z[..., Q_DIM:Q_DIM + KV_DIM].reshape(B, T, N_KV_HEADS, HEAD_DIM)
    v = z[..., Q_DIM + KV_DIM:Q_DIM + 2 * KV_DIM].reshape(B, T, N_KV_HEADS, HEAD_DIM)
    u = z[..., Q_DIM + 2 * KV_DIM:]
    a = attn_fn(q, k, v)
    b, new_pool = multiscale_pool(u, pool_hist, pos0, w_pool, pool_scale)
    mix = jnp.concatenate([rmsnorm(a, g_attn_out), rmsnorm(b, g_pool_out)], axis=-1)
    x = x + jnp.einsum('bte,ed->btd', mix, w_out)
    hf = rmsnorm(x, ln_ffn)
    x = x + jnp.einsum('btf,fd->btd', jnp.square(jax.nn.relu(jnp.einsum('btd,df->btf', hf, w_up))), w_down)
    gate = jax.nn.sigmoid(jnp.einsum('btd,de->bte', rmsnorm(x, ln_ple), w_ple_gate))
    x = x + gate * jnp.einsum('btp,pd->btd', ple, w_ple_proj)
    return x, k, v, new_pool


def setup_inputs(seed: int = 0) -> dict:
    key = jax.random.key(seed)
    ks = jax.random.split(key, 24)
    f32 = jnp.float32

    def nrm(k, shape, scale=1.0):
        return jax.random.normal(k, shape, f32) * scale

    def gain(k, shape):
        return 1.0 + 0.05 * jax.random.normal(k, shape, f32)

    return {
        "x_prompt": nrm(ks[0], (BATCH, SEQ, D_MODEL)),
        "x_sample": nrm(ks[1], (DEC_BATCH, DEC_SEQ, D_MODEL)),
        "cache_k": nrm(ks[2], (DEPTH, DEC_BATCH, WINDOW, N_KV_HEADS, HEAD_DIM)),
        "cache_v": nrm(ks[3], (DEPTH, DEC_BATCH, WINDOW, N_KV_HEADS, HEAD_DIM)),
        "state_pool": nrm(ks[4], (DEPTH, DEC_BATCH, POOL_HIST, POOL_WIDTH)),
        "p_prompt": nrm(ks[5], (DEPTH, BATCH, SEQ, PLE_DIM)),
        "p_sample": nrm(ks[6], (DEPTH, DEC_BATCH, DEC_SEQ, PLE_DIM)),
        "ln_mix": gain(ks[7], (DEPTH, D_MODEL)),
        "w_in": nrm(ks[8], (DEPTH, D_MODEL, IN_PROJ_DIM), D_MODEL ** -0.5),
        "attn_sinks": nrm(ks[9], (DEPTH, N_HEADS), 0.5),
        "g_attn_out": gain(ks[10], (DEPTH, ATTN_WIDTH)),
        "w_pool": nrm(ks[11], (DEPTH, N_POOL_GROUPS, POOL_GROUP_DIM, POOL_GROUP_DIM), POOL_GROUP_DIM ** -0.5),
        "pool_scale": 0.5 + 0.05 * jax.random.normal(ks[12], (DEPTH, POOL_WIDTH), f32),
        "g_pool_out": gain(ks[13], (DEPTH, POOL_WIDTH)),
        "w_out": nrm(ks[14], (DEPTH, D_MODEL, D_MODEL), D_MODEL ** -0.5),
        "ln_ffn": gain(ks[15], (DEPTH, D_MODEL)),
        "w_up": nrm(ks[16], (DEPTH, D_MODEL, D_FF), D_MODEL ** -0.5),
        "w_down": nrm(ks[17], (DEPTH, D_FF, D_MODEL), D_FF ** -0.5),
        "ln_ple": gain(ks[18], (DEPTH, D_MODEL)),
        "w_ple_gate": nrm(ks[19], (DEPTH, D_MODEL, D_MODEL), D_MODEL ** -0.5),
        "w_ple_proj": nrm(ks[20], (DEPTH, PLE_DIM, D_MODEL), PLE_DIM ** -0.5),
        "ln_final": gain(ks[21], (D_MODEL,)),
    }


def reference(x_prompt, x_sample, cache_k, cache_v, state_pool, p_prompt, p_sample,
              ln_mix, w_in, attn_sinks, g_attn_out, w_pool, pool_scale, g_pool_out,
              w_out, ln_ffn, w_up, w_down, ln_ple, w_ple_gate, w_ple_proj, ln_final):
    xp, xs = x_prompt, x_sample
    kp_l, vp_l, pp_l, ks_l, vs_l, ps_l = [], [], [], [], [], []
    for i in range(DEPTH):
        shared = (ln_mix[i], w_in[i], g_attn_out[i], w_pool[i], pool_scale[i], g_pool_out[i],
                  w_out[i], ln_ffn[i], w_up[i], w_down[i], ln_ple[i], w_ple_gate[i], w_ple_proj[i])
        fn_p = functools.partial(attn_prompt, sinks=attn_sinks[i])
        hist_p = jnp.zeros((xp.shape[0], POOL_HIST, POOL_WIDTH), xp.dtype)
        xp, k_p, v_p, pool_p = trunk_layer(xp, p_prompt[i], fn_p, hist_p, 0, *shared)
        fn_s = functools.partial(attn_sample, cache_k=cache_k[i], cache_v=cache_v[i], sinks=attn_sinks[i])
        xs, k_s, v_s, pool_s = trunk_layer(xs, p_sample[i], fn_s, state_pool[i], PAST_LEN, *shared)
        kp_l.append(k_p[:, -WINDOW:])
        vp_l.append(v_p[:, -WINDOW:])
        pp_l.append(pool_p)
        ks_l.append(k_s)
        vs_l.append(v_s)
        ps_l.append(pool_s)
    y_prompt = rmsnorm(xp, ln_final)
    y_sample = rmsnorm(xs, ln_final)
    new_k_prompt = jnp.stack(kp_l)
    new_v_prompt = jnp.stack(vp_l)
    new_pool_prompt = jnp.stack(pp_l)
    new_k_sample = jnp.stack(ks_l)
    new_v_sample = jnp.stack(vs_l)
    new_pool_sample = jnp.stack(ps_l)
    return (y_prompt, y_sample, new_k_prompt, new_v_prompt, new_pool_prompt,
            new_k_sample, new_v_sample, new_pool_sample)
```

```python
import functools

import jax
import jax.numpy as jnp
from jax import lax
from jax.experimental import pallas as pl
from jax.experimental.pallas import tpu as pltpu

D_MODEL = 1024
CHUNK = 64
WINDOW = 128
N_HEADS = 8
HEAD_DIM = 64
N_KV_HEADS = 2
GQA_GROUP = N_HEADS // N_KV_HEADS
Q_DIM = N_HEADS * HEAD_DIM
KV_DIM = N_KV_HEADS * HEAD_DIM
POOL_WIDTH = 512
POOL_WINDOWS = (2, 4, 8, 16)
POOL_GROUP_DIM = POOL_WIDTH // len(POOL_WINDOWS)
POOL_HIST = max(POOL_WINDOWS) - 1
HIST_ROWS = 16
IN_PROJ_DIM = Q_DIM + 2 * KV_DIM + POOL_WIDTH
D_FF = 4 * D_MODEL
PLE_DIM = 256
PAST_LEN = 2048
RMS_EPS = 1e-6
LANES = 128
HALF = LANES // 2

SEQ_TILE = 512
FF_CHUNK = 1024
VMEM_LIMIT_BYTES = 58 * 1024 * 1024

_F32 = jnp.float32
_BF16 = jnp.bfloat16


def _rms(x, g):
    ms = jnp.mean(x * x, axis=-1, keepdims=True)
    return x * lax.rsqrt(ms + RMS_EPS) * g


def _dot(a, w):
    return jnp.dot(a.astype(_BF16), w, preferred_element_type=_F32)


def _dot_nt(a, b):
    return lax.dot_general(a, b, (((1,), (1,)), ((), ())), preferred_element_type=_F32)


def _stack_heads(q_rows):
    lane = lax.broadcasted_iota(jnp.int32, (q_rows.shape[0], LANES), 1)
    lo, hi = [], []
    zero = jnp.zeros((q_rows.shape[0], LANES), q_rows.dtype)
    for p in range(GQA_GROUP):
        blk = q_rows[:, p * LANES:(p + 1) * LANES]
        lo.append(jnp.where(lane < HALF, blk, zero))
        hi.append(jnp.where(lane >= HALF, blk, zero))
    return jnp.concatenate(lo + hi, axis=0)


def _unstack_heads(o, rows):
    lane = lax.broadcasted_iota(jnp.int32, (rows, LANES), 1)
    blocks = []
    for p in range(GQA_GROUP):
        o_lo = o[p * rows:(p + 1) * rows]
        o_hi = o[(GQA_GROUP + p) * rows:(GQA_GROUP + p + 1) * rows]
        blocks.append(jnp.where(lane < HALF, o_lo, o_hi))
    return jnp.concatenate(blocks, axis=-1)


def _sink_softmax_pv(score_parts, value_parts, sink_col):
    m = sink_col
    for s in score_parts:
        m = jnp.maximum(m, jnp.max(s, axis=-1, keepdims=True))
    den = jnp.exp(sink_col - m)
    o = None
    for s, v in zip(score_parts, value_parts):
        e = jnp.exp(s - m)
        den = den + jnp.sum(e, axis=-1, keepdims=True)
        pv = jnp.dot(e.astype(_BF16), v, preferred_element_type=_F32)
        o = pv if o is None else o + pv
    return o * (1.0 / den)


def _pool_windows(ext, n_hist):
    del n_hist
    outs = []
    for g, w in enumerate(POOL_WINDOWS):
        acc = ext[:, g * POOL_GROUP_DIM:(g + 1) * POOL_GROUP_DIM]
        span = 1
        while span < w:
            acc = acc + pltpu.roll(acc, span, 0)
            span *= 2
        outs.append(acc)
    return outs


def _pool_mix(sums, u, inv_cnt, w_pool_ref, pool_scale):
    outs = []
    for g in range(len(POOL_WINDOWS)):
        cols = slice(g * POOL_GROUP_DIM, (g + 1) * POOL_GROUP_DIM)
        d = sums[g] * inv_cnt[g] - u[:, cols]
        outs.append(_dot(d, w_pool_ref[g]))
    return jnp.concatenate(outs, axis=-1) * pool_scale


def _dense_tail(x, a, b, p, g_attn_ref, g_pool_ref, w_out_ref, ln_ffn_ref, w_up_ref, w_down_ref,
                ln_ple_ref, w_gate_ref, w_ple_ref, ln_final_ref):
    mix = jnp.concatenate([_rms(a, g_attn_ref[...]), _rms(b, g_pool_ref[...])], axis=-1)
    x = x + _dot(mix, w_out_ref[...])
    hf = (_rms(x, ln_ffn_ref[...])).astype(_BF16)
    acc = None
    for c in range(D_FF // FF_CHUNK):
        up = jnp.dot(hf, w_up_ref[:, c * FF_CHUNK:(c + 1) * FF_CHUNK], preferred_element_type=_F32)
        act = jnp.square(jnp.maximum(up, 0.0))
        dn = _dot(act, w_down_ref[c * FF_CHUNK:(c + 1) * FF_CHUNK, :])
        acc = dn if acc is None else acc + dn
    x = x + acc
    gate = jax.nn.sigmoid(_dot(_rms(x, ln_ple_ref[...]), w_gate_ref[...]))
    x = x + gate * _dot(p, w_ple_ref[...])
    return _rms(x, ln_final_ref[...])


def _prompt_kernel(x_ref, p_ref, sink_ref, ln_mix_ref, w_in_ref, g_attn_ref, w_pool_ref,
                   pool_scale_ref, g_pool_ref, w_out_ref, ln_ffn_ref, w_up_ref, w_down_ref,
                   ln_ple_ref, w_gate_ref, w_ple_ref, ln_final_ref,
                   y_ref, k_out_ref, v_out_ref, pool_out_ref,
                   kbuf, vbuf, ubuf, abuf, *, n_tiles):
    i = pl.program_id(1)
    T = SEQ_TILE

    @pl.when(i == 0)
    def _():
        kbuf[0:WINDOW, :] = jnp.zeros((WINDOW, KV_DIM), _BF16)
        vbuf[0:WINDOW, :] = jnp.zeros((WINDOW, KV_DIM), _BF16)
        ubuf[0:HIST_ROWS, :] = jnp.zeros((HIST_ROWS, POOL_WIDTH), _F32)

    x = x_ref[...]
    z = _dot(_rms(x, ln_mix_ref[...]), w_in_ref[...])
    q = (z[:, :Q_DIM] * (HEAD_DIM ** -0.5)).astype(_BF16)
    k = z[:, Q_DIM:Q_DIM + KV_DIM]
    v = z[:, Q_DIM + KV_DIM:Q_DIM + 2 * KV_DIM]
    u = z[:, Q_DIM + 2 * KV_DIM:]
    kbuf[WINDOW:WINDOW + T, :] = k.astype(_BF16)
    vbuf[WINDOW:WINDOW + T, :] = v.astype(_BF16)
    ubuf[HIST_ROWS:HIST_ROWS + T, :] = u

    sink_col = sink_ref[...]
    key_len = WINDOW + CHUNK
    for ci in range(T // CHUNK):
        r0 = ci * CHUNK
        s = _dot_nt(_stack_heads(q[r0:r0 + CHUNK]), kbuf[r0:r0 + key_len, :])
        if r0 < WINDOW:
            first_valid = jnp.where(i == 0, WINDOW - r0, 0)
            col = lax.broadcasted_iota(jnp.int32, s.shape, 1)
            s = jnp.where(col >= first_valid, s, -jnp.inf)
        o = _sink_softmax_pv([s], [vbuf[r0:r0 + key_len, :]], sink_col)
        abuf[r0:r0 + CHUNK, :] = _unstack_heads(o, CHUNK)

    ext = ubuf[...]
    sums = [sg[HIST_ROWS:] for sg in _pool_windows(ext, HIST_ROWS)]
    pos1 = (i * T + 1 + lax.broadcasted_iota(jnp.int32, (T, 1), 0))
    inv_cnt = [1.0 / jnp.minimum(pos1, w).astype(_F32) for w in POOL_WINDOWS]
    b = _pool_mix(sums, u, inv_cnt, w_pool_ref, pool_scale_ref[...])

    y_ref[...] = _dense_tail(x, abuf[...], b, p_ref[...], g_attn_ref, g_pool_ref, w_out_ref,
                             ln_ffn_ref, w_up_ref, w_down_ref, ln_ple_ref, w_gate_ref, w_ple_ref,
                             ln_final_ref)

    kbuf[0:WINDOW, :] = kbuf[T:T + WINDOW, :]
    vbuf[0:WINDOW, :] = vbuf[T:T + WINDOW, :]
    ubuf[0:HIST_ROWS, :] = ubuf[T:T + HIST_ROWS, :]

    @pl.when(i == n_tiles - 1)
    def _():
        k_out_ref[...] = k[T - WINDOW:]
        v_out_ref[...] = v[T - WINDOW:]
        pool_out_ref[...] = u[T - HIST_ROWS:]


def _sample_kernel(x_ref, p_ref, ck_ref, cv_ref, hist_ref, sink_ref, ln_mix_ref, w_in_ref,
                   g_attn_ref, w_pool_ref, pool_scale_ref, g_pool_ref, w_out_ref, ln_ffn_ref,
                   w_up_ref, w_down_ref, ln_ple_ref, w_gate_ref, w_ple_ref, ln_final_ref,
                   y_ref, k_out_ref, v_out_ref, u_out_ref,
                   qbuf, kbuf, vbuf, abuf, *, n_streams, n_frames):
    x = x_ref[...]
    z = _dot(_rms(x, ln_mix_ref[...]), w_in_ref[...])
    k = z[:, Q_DIM:Q_DIM + KV_DIM]
    v = z[:, Q_DIM + KV_DIM:Q_DIM + 2 * KV_DIM]
    u = z[:, Q_DIM + 2 * KV_DIM:]
    qbuf[...] = (z[:, :Q_DIM] * (HEAD_DIM ** -0.5)).astype(_BF16)
    kbuf[...] = k.astype(_BF16)
    vbuf[...] = v.astype(_BF16)
    k_out_ref[...] = k
    v_out_ref[...] = v
    u_out_ref[...] = u

    sink_col = sink_ref[...]

    def stream_body(s_idx, carry):
        rows = pl.ds(pl.multiple_of(s_idx * n_frames, n_frames), n_frames)
        qs = _stack_heads(qbuf[rows, :])
        s_cache = _dot_nt(qs, ck_ref[s_idx].astype(_BF16))
        s_new = _dot_nt(qs, kbuf[rows, :])
        o = _sink_softmax_pv([s_cache, s_new], [cv_ref[s_idx].astype(_BF16), vbuf[rows, :]],
                             sink_col)
        abuf[rows, :] = _unstack_heads(o, n_frames)
        return carry

    lax.fori_loop(0, n_streams, stream_body, 0)

    ext = jnp.concatenate([hist_ref[...], u.reshape(n_streams, n_frames, POOL_WIDTH)], axis=1)
    ext = ext.reshape(n_streams * (HIST_ROWS + n_frames), POOL_WIDTH)
    sums = [sg.reshape(n_streams, HIST_ROWS + n_frames, POOL_GROUP_DIM)[:, HIST_ROWS:, :]
            .reshape(n_streams * n_frames, POOL_GROUP_DIM) for sg in _pool_windows(ext, HIST_ROWS)]
    inv_cnt = [1.0 / min(PAST_LEN + 1, w) for w in POOL_WINDOWS]
    b = _pool_mix(sums, u, inv_cnt, w_pool_ref, pool_scale_ref[...])

    y_ref[...] = _dense_tail(x, abuf[...], b, p_ref[...], g_attn_ref, g_pool_ref, w_out_ref,
                             ln_ffn_ref, w_up_ref, w_down_ref, ln_ple_ref, w_gate_ref, w_ple_ref,
                             ln_final_ref)


def _resident(shape):
    zeros = (0,) * len(shape)
    return pl.BlockSpec(shape, lambda *_: zeros, pipeline_mode=pl.Buffered(1))


def _layer_weights(ln_mix, w_in, attn_sinks, g_attn_out, w_pool, pool_scale, g_pool_out, w_out,
                   ln_ffn, w_up, w_down, ln_ple, w_ple_gate, w_ple_proj, ln_final):
    head_order = []
    for p in range(GQA_GROUP):
        head_order += [p, GQA_GROUP + p]
    perm = jnp.concatenate([jnp.arange(h * HEAD_DIM, (h + 1) * HEAD_DIM) for h in head_order])
    w_in_p = jnp.concatenate([w_in[:, perm], w_in[:, Q_DIM:]], axis=1).astype(_BF16)
    w_out_p = jnp.concatenate([w_out[perm, :], w_out[Q_DIM:, :]], axis=0).astype(_BF16)
    row = lambda t: t.reshape(1, -1).astype(_F32)
    return dict(
        sinks=attn_sinks.astype(_F32),
        ln_mix=row(ln_mix), w_in=w_in_p, g_attn=row(g_attn_out[perm]), w_pool=w_pool.astype(_BF16),
        pool_scale=row(pool_scale), g_pool=row(g_pool_out), w_out=w_out_p, ln_ffn=row(ln_ffn),
        w_up=w_up.astype(_BF16), w_down=w_down.astype(_BF16), ln_ple=row(ln_ple),
        w_gate=w_ple_gate.astype(_BF16), w_ple=w_ple_proj.astype(_BF16), ln_final=row(ln_final))


_WEIGHT_ORDER = ("ln_mix", "w_in", "g_attn", "w_pool", "pool_scale", "g_pool", "w_out", "ln_ffn",
                 "w_up", "w_down", "ln_ple", "w_gate", "w_ple", "ln_final")


def _prompt_layer(x, p, lw):
    B, S, _ = x.shape
    T = SEQ_TILE
    assert S % T == 0 and T % CHUNK == 0 and T >= WINDOW
    n_tiles = S // T
    sink_col = jnp.repeat(lw["sinks"], CHUNK).reshape(N_HEADS * CHUNK, 1)
    weights = [lw[n] for n in _WEIGHT_ORDER]
    tile = lambda width: pl.BlockSpec((None, T, width), lambda b, i: (b, i, 0))
    last = lambda rows, width: pl.BlockSpec((None, rows, width), lambda b, i: (b, 0, 0))
    return pl.pallas_call(
        functools.partial(_prompt_kernel, n_tiles=n_tiles),
        grid=(B, n_tiles),
        in_specs=[tile(D_MODEL), tile(PLE_DIM), _resident(sink_col.shape)]
                 + [_resident(w.shape) for w in weights],
        out_specs=[tile(D_MODEL), last(WINDOW, KV_DIM), last(WINDOW, KV_DIM),
                   last(HIST_ROWS, POOL_WIDTH)],
        out_shape=[jax.ShapeDtypeStruct((B, S, D_MODEL), _F32),
                   jax.ShapeDtypeStruct((B, WINDOW, KV_DIM), _F32),
                   jax.ShapeDtypeStruct((B, WINDOW, KV_DIM), _F32),
                   jax.ShapeDtypeStruct((B, HIST_ROWS, POOL_WIDTH), _F32)],
        scratch_shapes=[pltpu.VMEM((WINDOW + T, KV_DIM), _BF16),
                        pltpu.VMEM((WINDOW + T, KV_DIM), _BF16),
                        pltpu.VMEM((HIST_ROWS + T, POOL_WIDTH), _F32),
                        pltpu.VMEM((T, Q_DIM), _F32)],
        compiler_params=pltpu.CompilerParams(
            dimension_semantics=("arbitrary", "arbitrary"), vmem_limit_bytes=VMEM_LIMIT_BYTES),
        name="prompt_layer",
    )(x, p, sink_col, *weights)


def _sample_layer(x, p, cache_k, cache_v, state_pool, lw):
    Bs, Ts, _ = x.shape
    R = Bs * Ts
    assert Ts == HIST_ROWS, "new frames per stream must fill one 16-row block"
    sink_col = jnp.repeat(lw["sinks"], Ts).reshape(N_HEADS * Ts, 1)
    weights = [lw[n] for n in _WEIGHT_ORDER]
    hist = jnp.pad(state_pool, ((0, 0), (HIST_ROWS - POOL_HIST, 0), (0, 0)))
    ins = [x.reshape(R, D_MODEL), p.reshape(R, PLE_DIM), cache_k.reshape(Bs, WINDOW, KV_DIM),
           cache_v.reshape(Bs, WINDOW, KV_DIM), hist, sink_col] + weights
    full = lambda shape: pl.BlockSpec(shape, lambda i: (0,) * len(shape))
    return pl.pallas_call(
        functools.partial(_sample_kernel, n_streams=Bs, n_frames=Ts),
        grid=(1,),
        in_specs=[_resident(a.shape) for a in ins],
        out_specs=[full((R, D_MODEL)), full((R, KV_DIM)), full((R, KV_DIM)), full((R, POOL_WIDTH))],
        out_shape=[jax.ShapeDtypeStruct((R, D_MODEL), _F32),
                   jax.ShapeDtypeStruct((R, KV_DIM), _F32),
                   jax.ShapeDtypeStruct((R, KV_DIM), _F32),
                   jax.ShapeDtypeStruct((R, POOL_WIDTH), _F32)],
        scratch_shapes=[pltpu.VMEM((R, Q_DIM), _BF16), pltpu.VMEM((R, KV_DIM), _BF16),
                        pltpu.VMEM((R, KV_DIM), _BF16), pltpu.VMEM((R, Q_DIM), _F32)],
        compiler_params=pltpu.CompilerParams(
            dimension_semantics=("arbitrary",), vmem_limit_bytes=VMEM_LIMIT_BYTES),
        name="sample_layer",
    )(*ins)


def kernel(x_prompt, x_sample, cache_k, cache_v, state_pool, p_prompt, p_sample, ln_mix, w_in,
           attn_sinks, g_attn_out, w_pool, pool_scale, g_pool_out, w_out, ln_ffn, w_up, w_down,
           ln_ple, w_ple_gate, w_ple_proj, ln_final):
    depth = ln_mix.shape[0]
    assert depth == 1, "the final norm is fused into the single layer's kernel"
    B = x_prompt.shape[0]
    Bs, Ts, _ = x_sample.shape
    lw = _layer_weights(ln_mix[0], w_in[0], attn_sinks[0], g_attn_out[0], w_pool[0], pool_scale[0],
                        g_pool_out[0], w_out[0], ln_ffn[0], w_up[0], w_down[0], ln_ple[0],
                        w_ple_gate[0], w_ple_proj[0], ln_final)
    y_p, k_p, v_p, pool_p = _prompt_layer(x_prompt, p_prompt[0], lw)
    y_s, k_s, v_s, u_s = _sample_layer(x_sample, p_sample[0], cache_k[0], cache_v[0],
                                       state_pool[0], lw)
    kv_p = lambda t: t.reshape(1, B, WINDOW, N_KV_HEADS, HEAD_DIM)
    kv_s = lambda t: t.reshape(1, Bs, Ts, N_KV_HEADS, HEAD_DIM)
    return (y_p, y_s.reshape(Bs, Ts, D_MODEL), kv_p(k_p), kv_p(v_p),
            pool_p[None, :, HIST_ROWS - POOL_HIST:, :],
            kv_s(k_s), kv_s(v_s),
            u_s.reshape(Bs, Ts, POOL_WIDTH)[None, :, Ts - POOL_HIST:, :])
```

```python
import functools

import jax
import jax.numpy as jnp
from jax import lax
from jax.experimental import pallas as pl
from jax.experimental.pallas import tpu as pltpu

D_MODEL = 1024
CHUNK = 64
WINDOW = 128
N_HEADS = 8
HEAD_DIM = 64
N_KV_HEADS = 2
GQA_GROUP = N_HEADS // N_KV_HEADS
Q_DIM = N_HEADS * HEAD_DIM
KV_DIM = N_KV_HEADS * HEAD_DIM
POOL_WIDTH = 512
POOL_WINDOWS = (2, 4, 8, 16)
POOL_GROUP_DIM = POOL_WIDTH // len(POOL_WINDOWS)
POOL_HIST = max(POOL_WINDOWS) - 1
HIST_ROWS = 16
IN_PROJ_DIM = Q_DIM + 2 * KV_DIM + POOL_WIDTH
D_FF = 4 * D_MODEL
PLE_DIM = 256
PAST_LEN = 2048
RMS_EPS = 1e-6
LANES = 128
HALF = LANES // 2

SEQ_TILE = 512
FF_CHUNK = 1024
VMEM_LIMIT_BYTES = 58 * 1024 * 1024

_F32 = jnp.float32
_BF16 = jnp.bfloat16
_PAIR_HEAD_ORDER = tuple(h for p in range(GQA_GROUP) for h in (p, GQA_GROUP + p))


def _rms(x, g):
    ms = jnp.mean(x * x, axis=-1, keepdims=True)
    return x * lax.rsqrt(ms + RMS_EPS) * g


def _dot(a, w):
    return jnp.dot(a.astype(_BF16), w, preferred_element_type=_F32)


def _dot_nt(a, b):
    return lax.dot_general(a, b, (((1,), (1,)), ((), ())), preferred_element_type=_F32)


def _stack_heads(q_rows):
    lane = lax.broadcasted_iota(jnp.int32, (q_rows.shape[0], LANES), 1)
    lo, hi = [], []
    zero = jnp.zeros((q_rows.shape[0], LANES), q_rows.dtype)
    for p in range(GQA_GROUP):
        blk = q_rows[:, p * LANES:(p + 1) * LANES]
        lo.append(jnp.where(lane < HALF, blk, zero))
        hi.append(jnp.where(lane >= HALF, blk, zero))
    return jnp.concatenate(lo + hi, axis=0)


def _unstack_heads(o, rows):
    lane = lax.broadcasted_iota(jnp.int32, (rows, LANES), 1)
    blocks = []
    for p in range(GQA_GROUP):
        o_lo = o[p * rows:(p + 1) * rows]
        o_hi = o[(GQA_GROUP + p) * rows:(GQA_GROUP + p + 1) * rows]
        blocks.append(jnp.where(lane < HALF, o_lo, o_hi))
    return jnp.concatenate(blocks, axis=-1)


def _sink_softmax_pv(score_parts, value_parts, sink_col):
    m = sink_col
    for s in score_parts:
        m = jnp.maximum(m, jnp.max(s, axis=-1, keepdims=True))
    den = jnp.exp(sink_col - m)
    o = None
    for s, v in zip(score_parts, value_parts):
        e = jnp.exp(s - m)
        den = den + jnp.sum(e, axis=-1, keepdims=True)
        pv = jnp.dot(e.astype(_BF16), v, preferred_element_type=_F32)
        o = pv if o is None else o + pv
    return o * (1.0 / den)


def _attention_transposed(qT, kbuf, vTbuf, sink_row, first_tile, aT_ref):
    T = qT.shape[1]
    pair = 2 * CHUNK
    n_keys = WINDOW + pair
    n_cols = N_HEADS * pair
    zero = jnp.zeros((HEAD_DIM, pair), qT.dtype)
    lane = lax.broadcasted_iota(jnp.int32, (CHUNK, n_cols), 1) % pair
    neg = jnp.float32(-jnp.inf)
    for j in range(T // pair):
        cols = slice(j * pair, (j + 1) * pair)
        blocks = []
        for rb in range(N_HEADS):
            qh = qT[rb * HEAD_DIM:(rb + 1) * HEAD_DIM, cols]
            blocks.append(jnp.concatenate([qh, zero] if rb % 2 == 0 else [zero, qh], axis=0))
        rhs = jnp.concatenate(blocks, axis=1)
        sc = jnp.dot(kbuf[j * pair:j * pair + n_keys, :], rhs, preferred_element_type=_F32)
        s0 = jnp.where(lane >= CHUNK, neg, sc[0:CHUNK])
        s3 = jnp.where(lane < CHUNK, neg, sc[3 * CHUNK:4 * CHUNK])
        s1 = sc[CHUNK:2 * CHUNK]
        if j == 0:
            s0 = jnp.where(first_tile, neg, s0)
            s1 = jnp.where(first_tile, neg, s1)
        sc = jnp.concatenate([s0, s1, sc[2 * CHUNK:3 * CHUNK], s3], axis=0)
        m = jnp.maximum(jnp.max(sc, axis=0, keepdims=True), sink_row)
        e = jnp.exp(sc - m)
        den = jnp.sum(e, axis=0, keepdims=True) + jnp.exp(sink_row - m)
        oT = jnp.dot(vTbuf[:, j * pair:j * pair + n_keys], e.astype(_BF16),
                     preferred_element_type=_F32)
        oT = oT * (1.0 / den)
        outs = []
        for rb in range(N_HEADS):
            g = rb % 2
            outs.append(oT[g * HEAD_DIM:(g + 1) * HEAD_DIM, rb * pair:(rb + 1) * pair])
        aT_ref[:, cols] = jnp.concatenate(outs, axis=0)


def _pool_windows(ext):
    outs = []
    for g, w in enumerate(POOL_WINDOWS):
        acc = ext[:, g * POOL_GROUP_DIM:(g + 1) * POOL_GROUP_DIM]
        span = 1
        while span < w:
            acc = acc + pltpu.roll(acc, span, 0)
            span *= 2
        outs.append(acc)
    return outs


def _pool_mix(sums, u, inv_cnt, w_pool_ref, pool_scale):
    outs = []
    for g in range(len(POOL_WINDOWS)):
        cols = slice(g * POOL_GROUP_DIM, (g + 1) * POOL_GROUP_DIM)
        d = sums[g] * inv_cnt[g] - u[:, cols]
        outs.append(_dot(d, w_pool_ref[g]))
    return jnp.concatenate(outs, axis=-1) * pool_scale


def _dense_tail(x, a, b, p, g_attn_ref, g_pool_ref, w_out_ref, ln_ffn_ref, w_up_ref, w_down_ref,
                ln_ple_ref, w_gate_ref, w_ple_ref, ln_final_ref):
    mix = jnp.concatenate([_rms(a, g_attn_ref[...]), _rms(b, g_pool_ref[...])], axis=-1)
    x = x + _dot(mix, w_out_ref[...])
    hf = (_rms(x, ln_ffn_ref[...])).astype(_BF16)
    acc = None
    for c in range(D_FF // FF_CHUNK):
        up = jnp.dot(hf, w_up_ref[:, c * FF_CHUNK:(c + 1) * FF_CHUNK], preferred_element_type=_F32)
        act = jnp.square(jnp.maximum(up, 0.0))
        dn = _dot(act, w_down_ref[c * FF_CHUNK:(c + 1) * FF_CHUNK, :])
        acc = dn if acc is None else acc + dn
    x = x + acc
    gate = jax.nn.sigmoid(_dot(_rms(x, ln_ple_ref[...]), w_gate_ref[...]))
    x = x + gate * _dot(p, w_ple_ref[...])
    return _rms(x, ln_final_ref[...])


def _prompt_kernel(x_ref, p_ref, sink_ref, ln_mix_ref, w_in_ref, g_attn_ref, w_pool_ref,
                   pool_scale_ref, g_pool_ref, w_out_ref, ln_ffn_ref, w_up_ref, w_down_ref,
                   ln_ple_ref, w_gate_ref, w_ple_ref, ln_final_ref,
                   y_ref, k_out_ref, v_out_ref, pool_out_ref,
                   kbuf, vTbuf, ubuf, aT_buf, *, n_tiles):
    i = pl.program_id(1)
    T = SEQ_TILE

    @pl.when(i == 0)
    def _():
        kbuf[0:WINDOW, :] = jnp.zeros((WINDOW, KV_DIM), _BF16)
        vTbuf[:, 0:WINDOW] = jnp.zeros((KV_DIM, WINDOW), _BF16)
        ubuf[0:HIST_ROWS, :] = jnp.zeros((HIST_ROWS, POOL_WIDTH), _F32)

    x = x_ref[...]
    z = _dot(_rms(x, ln_mix_ref[...]), w_in_ref[...])
    q = z[:, :Q_DIM] * (HEAD_DIM ** -0.5)
    k = z[:, Q_DIM:Q_DIM + KV_DIM]
    v = z[:, Q_DIM + KV_DIM:Q_DIM + 2 * KV_DIM]
    u = z[:, Q_DIM + 2 * KV_DIM:]
    kbuf[WINDOW:WINDOW + T, :] = k.astype(_BF16)
    vTbuf[:, WINDOW:WINDOW + T] = v.T.astype(_BF16)
    ubuf[HIST_ROWS:HIST_ROWS + T, :] = u

    _attention_transposed(q.T.astype(_BF16), kbuf, vTbuf, sink_ref[...], i == 0, aT_buf)
    a = aT_buf[...].T

    sums = [sg[HIST_ROWS:] for sg in _pool_windows(ubuf[...])]
    pos1 = (i * T + 1 + lax.broadcasted_iota(jnp.int32, (T, 1), 0))
    inv_cnt = [1.0 / jnp.minimum(pos1, w).astype(_F32) for w in POOL_WINDOWS]
    b = _pool_mix(sums, u, inv_cnt, w_pool_ref, pool_scale_ref[...])

    y_ref[...] = _dense_tail(x, a, b, p_ref[...], g_attn_ref, g_pool_ref, w_out_ref,
                             ln_ffn_ref, w_up_ref, w_down_ref, ln_ple_ref, w_gate_ref, w_ple_ref,
                             ln_final_ref)

    kbuf[0:WINDOW, :] = kbuf[T:T + WINDOW, :]
    vTbuf[:, 0:WINDOW] = vTbuf[:, T:T + WINDOW]
    ubuf[0:HIST_ROWS, :] = ubuf[T:T + HIST_ROWS, :]

    @pl.when(i == n_tiles - 1)
    def _():
        k_out_ref[...] = k[T - WINDOW:]
        v_out_ref[...] = v[T - WINDOW:]
        pool_out_ref[...] = u[T - HIST_ROWS:]


def _sample_kernel(x_ref, p_ref, ck_ref, cv_ref, hist_ref, sink_ref, ln_mix_ref, w_in_ref,
                   g_attn_ref, w_pool_ref, pool_scale_ref, g_pool_ref, w_out_ref, ln_ffn_ref,
                   w_up_ref, w_down_ref, ln_ple_ref, w_gate_ref, w_ple_ref, ln_final_ref,
                   y_ref, k_out_ref, v_out_ref, u_out_ref,
                   qbuf, kbuf, vbuf, abuf, *, n_streams, n_frames):
    x = x_ref[...]
    z = _dot(_rms(x, ln_mix_ref[...]), w_in_ref[...])
    k = z[:, Q_DIM:Q_DIM + KV_DIM]
    v = z[:, Q_DIM + KV_DIM:Q_DIM + 2 * KV_DIM]
    u = z[:, Q_DIM + 2 * KV_DIM:]
    qbuf[...] = (z[:, :Q_DIM] * (HEAD_DIM ** -0.5)).astype(_BF16)
    kbuf[...] = k.astype(_BF16)
    vbuf[...] = v.astype(_BF16)
    k_out_ref[...] = k
    v_out_ref[...] = v
    u_out_ref[...] = u

    sink_col = sink_ref[...]

    def stream_body(s_idx, carry):
        rows = pl.ds(pl.multiple_of(s_idx * n_frames, n_frames), n_frames)
        qs = _stack_heads(qbuf[rows, :])
        s_cache = _dot_nt(qs, ck_ref[s_idx].astype(_BF16))
        s_new = _dot_nt(qs, kbuf[rows, :])
        o = _sink_softmax_pv([s_cache, s_new], [cv_ref[s_idx].astype(_BF16), vbuf[rows, :]],
                             sink_col)
        abuf[rows, :] = _unstack_heads(o, n_frames)
        return carry

    lax.fori_loop(0, n_streams, stream_body, 0)

    ext = jnp.concatenate([hist_ref[...], u.reshape(n_streams, n_frames, POOL_WIDTH)], axis=1)
    ext = ext.reshape(n_streams * (HIST_ROWS + n_frames), POOL_WIDTH)
    sums = [sg.reshape(n_streams, HIST_ROWS + n_frames, POOL_GROUP_DIM)[:, HIST_ROWS:, :]
            .reshape(n_streams * n_frames, POOL_GROUP_DIM) for sg in _pool_windows(ext)]
    inv_cnt = [1.0 / min(PAST_LEN + 1, w) for w in POOL_WINDOWS]
    b = _pool_mix(sums, u, inv_cnt, w_pool_ref, pool_scale_ref[...])

    y_ref[...] = _dense_tail(x, abuf[...], b, p_ref[...], g_attn_ref, g_pool_ref, w_out_ref,
                             ln_ffn_ref, w_up_ref, w_down_ref, ln_ple_ref, w_gate_ref, w_ple_ref,
                             ln_final_ref)


def _resident(shape):
    zeros = (0,) * len(shape)
    return pl.BlockSpec(shape, lambda *_: zeros, pipeline_mode=pl.Buffered(1))


def _layer_weights(ln_mix, w_in, attn_sinks, g_attn_out, w_pool, pool_scale, g_pool_out, w_out,
                   ln_ffn, w_up, w_down, ln_ple, w_ple_gate, w_ple_proj, ln_final):
    perm = jnp.concatenate([jnp.arange(h * HEAD_DIM, (h + 1) * HEAD_DIM)
                            for h in _PAIR_HEAD_ORDER])
    w_in_p = jnp.concatenate([w_in[:, perm], w_in[:, Q_DIM:]], axis=1).astype(_BF16)
    w_out_p = jnp.concatenate([w_out[perm, :], w_out[Q_DIM:, :]], axis=0).astype(_BF16)
    row = lambda t: t.reshape(1, -1).astype(_F32)
    return dict(
        sinks=attn_sinks.astype(_F32),
        ln_mix=row(ln_mix), w_in=w_in_p, g_attn=row(g_attn_out[perm]), w_pool=w_pool.astype(_BF16),
        pool_scale=row(pool_scale), g_pool=row(g_pool_out), w_out=w_out_p, ln_ffn=row(ln_ffn),
        w_up=w_up.astype(_BF16), w_down=w_down.astype(_BF16), ln_ple=row(ln_ple),
        w_gate=w_ple_gate.astype(_BF16), w_ple=w_ple_proj.astype(_BF16), ln_final=row(ln_final))


_WEIGHT_ORDER = ("ln_mix", "w_in", "g_attn", "w_pool", "pool_scale", "g_pool", "w_out", "ln_ffn",
                 "w_up", "w_down", "ln_ple", "w_gate", "w_ple", "ln_final")


def _prompt_layer(x, p, lw):
    B, S, _ = x.shape
    T = SEQ_TILE
    assert S % T == 0 and T % (2 * CHUNK) == 0 and T >= WINDOW
    n_tiles = S // T
    sink_row = jnp.repeat(lw["sinks"][jnp.array(_PAIR_HEAD_ORDER)], 2 * CHUNK).reshape(1, -1)
    weights = [lw[n] for n in _WEIGHT_ORDER]
    tile = lambda width: pl.BlockSpec((None, T, width), lambda b, i: (b, i, 0))
    last = lambda rows, width: pl.BlockSpec((None, rows, width), lambda b, i: (b, 0, 0))
    return pl.pallas_call(
        functools.partial(_prompt_kernel, n_tiles=n_tiles),
        grid=(B, n_tiles),
        in_specs=[tile(D_MODEL), tile(PLE_DIM), _resident(sink_row.shape)]
                 + [_resident(w.shape) for w in weights],
        out_specs=[tile(D_MODEL), last(WINDOW, KV_DIM), last(WINDOW, KV_DIM),
                   last(HIST_ROWS, POOL_WIDTH)],
        out_shape=[jax.ShapeDtypeStruct((B, S, D_MODEL), _F32),
                   jax.ShapeDtypeStruct((B, WINDOW, KV_DIM), _F32),
                   jax.ShapeDtypeStruct((B, WINDOW, KV_DIM), _F32),
                   jax.ShapeDtypeStruct((B, HIST_ROWS, POOL_WIDTH), _F32)],
        scratch_shapes=[pltpu.VMEM((WINDOW + T, KV_DIM), _BF16),
                        pltpu.VMEM((KV_DIM, WINDOW + T), _BF16),
                        pltpu.VMEM((HIST_ROWS + T, POOL_WIDTH), _F32),
                        pltpu.VMEM((Q_DIM, T), _F32)],
        compiler_params=pltpu.CompilerParams(
            dimension_semantics=("arbitrary", "arbitrary"), vmem_limit_bytes=VMEM_LIMIT_BYTES),
        name="prompt_layer",
    )(x, p, sink_row, *weights)


def _sample_layer(x, p, cache_k, cache_v, state_pool, lw):
    Bs, Ts, _ = x.shape
    R = Bs * Ts
    assert Ts == HIST_ROWS, "new frames per stream must fill one 16-row block"
    sink_col = jnp.repeat(lw["sinks"], Ts).reshape(N_HEADS * Ts, 1)
    weights = [lw[n] for n in _WEIGHT_ORDER]
    hist = jnp.pad(state_pool, ((0, 0), (HIST_ROWS - POOL_HIST, 0), (0, 0)))
    ins = [x.reshape(R, D_MODEL), p.reshape(R, PLE_DIM), cache_k.reshape(Bs, WINDOW, KV_DIM),
           cache_v.reshape(Bs, WINDOW, KV_DIM), hist, sink_col] + weights
    full = lambda shape: pl.BlockSpec(shape, lambda i: (0,) * len(shape))
    return pl.pallas_call(
        functools.partial(_sample_kernel, n_streams=Bs, n_frames=Ts),
        grid=(1,),
        in_specs=[_resident(a.shape) for a in ins],
        out_specs=[full((R, D_MODEL)), full((R, KV_DIM)), full((R, KV_DIM)), full((R, POOL_WIDTH))],
        out_shape=[jax.ShapeDtypeStruct((R, D_MODEL), _F32),
                   jax.ShapeDtypeStruct((R, KV_DIM), _F32),
                   jax.ShapeDtypeStruct((R, KV_DIM), _F32),
                   jax.ShapeDtypeStruct((R, POOL_WIDTH), _F32)],
        scratch_shapes=[pltpu.VMEM((R, Q_DIM), _BF16), pltpu.VMEM((R, KV_DIM), _BF16),
                        pltpu.VMEM((R, KV_DIM), _BF16), pltpu.VMEM((R, Q_DIM), _F32)],
        compiler_params=pltpu.CompilerParams(
            dimension_semantics=("arbitrary",), vmem_limit_bytes=VMEM_LIMIT_BYTES),
        name="sample_layer",
    )(*ins)


def kernel(x_prompt, x_sample, cache_k, cache_v, state_pool, p_prompt, p_sample, ln_mix, w_in,
           attn_sinks, g_attn_out, w_pool, pool_scale, g_pool_out, w_out, ln_ffn, w_up, w_down,
           ln_ple, w_ple_gate, w_ple_proj, ln_final):
    depth = ln_mix.shape[0]
    assert depth == 1, "the final norm is fused into the single layer's kernel"
    B = x_prompt.shape[0]
    Bs, Ts, _ = x_sample.shape
    lw = _layer_weights(ln_mix[0], w_in[0], attn_sinks[0], g_attn_out[0], w_pool[0], pool_scale[0],
                        g_pool_out[0], w_out[0], ln_ffn[0], w_up[0], w_down[0], ln_ple[0],
                        w_ple_gate[0], w_ple_proj[0], ln_final)
    y_p, k_p, v_p, pool_p = _prompt_layer(x_prompt, p_prompt[0], lw)
    y_s, k_s, v_s, u_s = _sample_layer(x_sample, p_sample[0], cache_k[0], cache_v[0],
                                       state_pool[0], lw)
    kv_p = lambda t: t.reshape(1, B, WINDOW, N_KV_HEADS, HEAD_DIM)
    kv_s = lambda t: t.reshape(1, Bs, Ts, N_KV_HEADS, HEAD_DIM)
    return (y_p, y_s.reshape(Bs, Ts, D_MODEL), kv_p(k_p), kv_p(v_p),
            pool_p[None, :, HIST_ROWS - POOL_HIST:, :],
            kv_s(k_s), kv_s(v_s),
            u_s.reshape(Bs, Ts, POOL_WIDTH)[None, :, Ts - POOL_HIST:, :])
```

```python
import functools

import jax
import jax.numpy as jnp
from jax import lax
from jax.experimental import pallas as pl
from jax.experimental.pallas import tpu as pltpu

D_MODEL = 1024
CHUNK = 64
WINDOW = 128
N_HEADS = 8
HEAD_DIM = 64
N_KV_HEADS = 2
GQA_GROUP = N_HEADS // N_KV_HEADS
Q_DIM = N_HEADS * HEAD_DIM
KV_DIM = N_KV_HEADS * HEAD_DIM
POOL_WIDTH = 512
POOL_WINDOWS = (2, 4, 8, 16)
POOL_GROUP_DIM = POOL_WIDTH // len(POOL_WINDOWS)
POOL_HIST = max(POOL_WINDOWS) - 1
HIST_ROWS = 16
IN_PROJ_DIM = Q_DIM + 2 * KV_DIM + POOL_WIDTH
D_FF = 4 * D_MODEL
PLE_DIM = 256
PAST_LEN = 2048
RMS_EPS = 1e-6
LANES = 128
HALF = LANES // 2

SEQ_TILE = 512
FF_CHUNK = 512
VMEM_LIMIT_BYTES = 58 * 1024 * 1024

_F32 = jnp.float32
_BF16 = jnp.bfloat16
_PAIR_HEAD_ORDER = tuple(h for p in range(GQA_GROUP) for h in (p, GQA_GROUP + p))


def _rms(x, g):
    ms = jnp.mean(x * x, axis=-1, keepdims=True)
    return x * lax.rsqrt(ms + RMS_EPS) * g


def _dot(a, w):
    return jnp.dot(a.astype(_BF16), w, preferred_element_type=_F32)


def _dot_nt(a, b):
    return lax.dot_general(a, b, (((1,), (1,)), ((), ())), preferred_element_type=_F32)


def _stack_heads(q_rows):
    lane = lax.broadcasted_iota(jnp.int32, (q_rows.shape[0], LANES), 1)
    lo, hi = [], []
    zero = jnp.zeros((q_rows.shape[0], LANES), q_rows.dtype)
    for p in range(GQA_GROUP):
        blk = q_rows[:, p * LANES:(p + 1) * LANES]
        lo.append(jnp.where(lane < HALF, blk, zero))
        hi.append(jnp.where(lane >= HALF, blk, zero))
    return jnp.concatenate(lo + hi, axis=0)


def _unstack_heads(o, rows):
    lane = lax.broadcasted_iota(jnp.int32, (rows, LANES), 1)
    blocks = []
    for p in range(GQA_GROUP):
        o_lo = o[p * rows:(p + 1) * rows]
        o_hi = o[(GQA_GROUP + p) * rows:(GQA_GROUP + p + 1) * rows]
        blocks.append(jnp.where(lane < HALF, o_lo, o_hi))
    return jnp.concatenate(blocks, axis=-1)


def _sink_softmax_pv(score_parts, value_parts, sink_col):
    m = sink_col
    for s in score_parts:
        m = jnp.maximum(m, jnp.max(s, axis=-1, keepdims=True))
    den = jnp.exp(sink_col - m)
    o = None
    for s, v in zip(score_parts, value_parts):
        e = jnp.exp(s - m)
        den = den + jnp.sum(e, axis=-1, keepdims=True)
        pv = jnp.dot(e.astype(_BF16), v, preferred_element_type=_F32)
        o = pv if o is None else o + pv
    return o * (1.0 / den)


def _attention_transposed(qT, kbuf, vTbuf, sink_row, first_tile, aT_ref, interleave=None):
    T = qT.shape[1]
    pair = 2 * CHUNK
    n_keys = WINDOW + pair
    n_cols = N_HEADS * pair
    zero = jnp.zeros((HEAD_DIM, pair), qT.dtype)
    lane = lax.broadcasted_iota(jnp.int32, (CHUNK, n_cols), 1) % pair
    neg = jnp.float32(-jnp.inf)
    for j in range(T // pair):
        cols = slice(j * pair, (j + 1) * pair)
        blocks = []
        for rb in range(N_HEADS):
            qh = qT[rb * HEAD_DIM:(rb + 1) * HEAD_DIM, cols]
            blocks.append(jnp.concatenate([qh, zero] if rb % 2 == 0 else [zero, qh], axis=0))
        rhs = jnp.concatenate(blocks, axis=1)
        sc = jnp.dot(kbuf[j * pair:j * pair + n_keys, :], rhs, preferred_element_type=_F32)
        s0 = jnp.where(lane >= CHUNK, neg, sc[0:CHUNK])
        s3 = jnp.where(lane < CHUNK, neg, sc[3 * CHUNK:4 * CHUNK])
        s1 = sc[CHUNK:2 * CHUNK]
        if j == 0:
            s0 = jnp.where(first_tile, neg, s0)
            s1 = jnp.where(first_tile, neg, s1)
        sc = jnp.concatenate([s0, s1, sc[2 * CHUNK:3 * CHUNK], s3], axis=0)
        m = jnp.maximum(jnp.max(sc, axis=0, keepdims=True), sink_row)
        if interleave is not None:
            interleave(2 * j)
        e = jnp.exp(sc - m)
        den = jnp.sum(e, axis=0, keepdims=True) + jnp.exp(sink_row - m)
        oT = jnp.dot(vTbuf[:, j * pair:j * pair + n_keys], e.astype(_BF16),
                     preferred_element_type=_F32)
        oT = oT * (1.0 / den)
        outs = []
        for rb in range(N_HEADS):
            g = rb % 2
            outs.append(oT[g * HEAD_DIM:(g + 1) * HEAD_DIM, rb * pair:(rb + 1) * pair])
        aT_ref[:, cols] = jnp.concatenate(outs, axis=0)
        if interleave is not None:
            interleave(2 * j + 1)


def _pool_windows(ext):
    outs = []
    for g, w in enumerate(POOL_WINDOWS):
        acc = ext[:, g * POOL_GROUP_DIM:(g + 1) * POOL_GROUP_DIM]
        span = 1
        while span < w:
            acc = acc + pltpu.roll(acc, span, 0)
            span *= 2
        outs.append(acc)
    return outs


def _pool_mix(sums, u, inv_cnt, w_pool_ref, pool_scale):
    outs = []
    for g in range(len(POOL_WINDOWS)):
        cols = slice(g * POOL_GROUP_DIM, (g + 1) * POOL_GROUP_DIM)
        d = sums[g] * inv_cnt[g] - u[:, cols]
        outs.append(_dot(d, w_pool_ref[g]))
    return jnp.concatenate(outs, axis=-1) * pool_scale


def _mix_out(x, a, b, g_attn_ref, g_pool_ref, w_out_ref, ln_ffn_ref):
    mix = jnp.concatenate([_rms(a, g_attn_ref[...]), _rms(b, g_pool_ref[...])], axis=-1)
    x = x + _dot(mix, w_out_ref[...])
    return x, _rms(x, ln_ffn_ref[...]).astype(_BF16)


def _ffn_chunk(hf, c, acc, w_up_ref, w_down_ref):
    cols = slice(c * FF_CHUNK, (c + 1) * FF_CHUNK)
    up = jnp.dot(hf, w_up_ref[:, cols], preferred_element_type=_F32)
    dn = _dot(jnp.square(jnp.maximum(up, 0.0)), w_down_ref[cols, :])
    return dn if acc is None else acc + dn


def _mlp_finish(x, p, ln_ple_ref, w_gate_ref, w_ple_ref, ln_final_ref):
    gate = jax.nn.sigmoid(_dot(_rms(x, ln_ple_ref[...]), w_gate_ref[...]))
    x = x + gate * _dot(p, w_ple_ref[...])
    return _rms(x, ln_final_ref[...])


def _prompt_kernel(x_ref, p_ref, sink_ref, ln_mix_ref, w_in_ref, g_attn_ref, w_pool_ref,
                   pool_scale_ref, g_pool_ref, w_out_ref, ln_ffn_ref, w_up_ref, w_down_ref,
                   ln_ple_ref, w_gate_ref, w_ple_ref, ln_final_ref,
                   y_ref, k_out_ref, v_out_ref, pool_out_ref,
                   kbuf, vTbuf, ubuf, aT_buf, x1_carry, hf_carry, *, n_tiles, n_total):
    s = pl.program_id(0)
    i = lax.rem(jnp.minimum(s, n_total - 1), n_tiles)
    T = SEQ_TILE

    @pl.when(s == 0)
    def _():
        x1_carry[...] = jnp.zeros(x1_carry.shape, _F32)
        hf_carry[...] = jnp.zeros(hf_carry.shape, _BF16)

    @pl.when(i == 0)
    def _():
        kbuf[0:WINDOW, :] = jnp.zeros((WINDOW, KV_DIM), _BF16)
        vTbuf[:, 0:WINDOW] = jnp.zeros((KV_DIM, WINDOW), _BF16)
        ubuf[0:HIST_ROWS, :] = jnp.zeros((HIST_ROWS, POOL_WIDTH), _F32)

    ffn_acc = [None]

    def ffn_chunk(c):
        ffn_acc[0] = _ffn_chunk(hf_carry[...], c, ffn_acc[0], w_up_ref, w_down_ref)

    assert D_FF // FF_CHUNK == 2 * (T // (2 * CHUNK))

    x = x_ref[...]
    z = _dot(_rms(x, ln_mix_ref[...]), w_in_ref[...])
    q = z[:, :Q_DIM] * (HEAD_DIM ** -0.5)
    k = z[:, Q_DIM:Q_DIM + KV_DIM]
    v = z[:, Q_DIM + KV_DIM:Q_DIM + 2 * KV_DIM]
    u = z[:, Q_DIM + 2 * KV_DIM:]
    kbuf[WINDOW:WINDOW + T, :] = k.astype(_BF16)
    vTbuf[:, WINDOW:WINDOW + T] = v.T.astype(_BF16)
    ubuf[HIST_ROWS:HIST_ROWS + T, :] = u

    _attention_transposed(q.T.astype(_BF16), kbuf, vTbuf, sink_ref[...], i == 0, aT_buf,
                          interleave=ffn_chunk)
    a = aT_buf[...].T

    sums = [sg[HIST_ROWS:] for sg in _pool_windows(ubuf[...])]
    pos1 = (i * T + 1 + lax.broadcasted_iota(jnp.int32, (T, 1), 0))
    inv_cnt = [1.0 / jnp.minimum(pos1, w).astype(_F32) for w in POOL_WINDOWS]
    b = _pool_mix(sums, u, inv_cnt, w_pool_ref, pool_scale_ref[...])
    mix = jnp.concatenate([_rms(a, g_attn_ref[...]), _rms(b, g_pool_ref[...])], axis=-1)

    x2 = x1_carry[...] + ffn_acc[0]
    n2 = _rms(x2, ln_ple_ref[...])
    x1 = x + _dot(mix, w_out_ref[...])
    gate_pre = _dot(n2, w_gate_ref[...])
    ple = _dot(p_ref[...], w_ple_ref[...])
    x1_carry[...] = x1
    hf_carry[...] = _rms(x1, ln_ffn_ref[...]).astype(_BF16)
    y_ref[...] = _rms(x2 + jax.nn.sigmoid(gate_pre) * ple, ln_final_ref[...])

    kbuf[0:WINDOW, :] = kbuf[T:T + WINDOW, :]
    vTbuf[:, 0:WINDOW] = vTbuf[:, T:T + WINDOW]
    ubuf[0:HIST_ROWS, :] = ubuf[T:T + HIST_ROWS, :]

    @pl.when(i == n_tiles - 1)
    def _():
        k_out_ref[...] = k[T - WINDOW:]
        v_out_ref[...] = v[T - WINDOW:]
        pool_out_ref[...] = u[T - HIST_ROWS:]


def _sample_kernel(x_ref, p_ref, ck_ref, cv_ref, hist_ref, sink_ref, ln_mix_ref, w_in_ref,
                   g_attn_ref, w_pool_ref, pool_scale_ref, g_pool_ref, w_out_ref, ln_ffn_ref,
                   w_up_ref, w_down_ref, ln_ple_ref, w_gate_ref, w_ple_ref, ln_final_ref,
                   y_ref, k_out_ref, v_out_ref, u_out_ref,
                   qbuf, kbuf, vbuf, abuf, *, n_streams, n_frames):
    x = x_ref[...]
    z = _dot(_rms(x, ln_mix_ref[...]), w_in_ref[...])
    k = z[:, Q_DIM:Q_DIM + KV_DIM]
    v = z[:, Q_DIM + KV_DIM:Q_DIM + 2 * KV_DIM]
    u = z[:, Q_DIM + 2 * KV_DIM:]
    qbuf[...] = (z[:, :Q_DIM] * (HEAD_DIM ** -0.5)).astype(_BF16)
    kbuf[...] = k.astype(_BF16)
    vbuf[...] = v.astype(_BF16)
    k_out_ref[...] = k
    v_out_ref[...] = v
    u_out_ref[...] = u

    sink_col = sink_ref[...]

    def stream_body(s_idx, carry):
        rows = pl.ds(pl.multiple_of(s_idx * n_frames, n_frames), n_frames)
        qs = _stack_heads(qbuf[rows, :])
        s_cache = _dot_nt(qs, ck_ref[s_idx].astype(_BF16))
        s_new = _dot_nt(qs, kbuf[rows, :])
        o = _sink_softmax_pv([s_cache, s_new], [cv_ref[s_idx].astype(_BF16), vbuf[rows, :]],
                             sink_col)
        abuf[rows, :] = _unstack_heads(o, n_frames)
        return carry

    lax.fori_loop(0, n_streams, stream_body, 0)

    ext = jnp.concatenate([hist_ref[...], u.reshape(n_streams, n_frames, POOL_WIDTH)], axis=1)
    ext = ext.reshape(n_streams * (HIST_ROWS + n_frames), POOL_WIDTH)
    sums = [sg.reshape(n_streams, HIST_ROWS + n_frames, POOL_GROUP_DIM)[:, HIST_ROWS:, :]
            .reshape(n_streams * n_frames, POOL_GROUP_DIM) for sg in _pool_windows(ext)]
    inv_cnt = [1.0 / min(PAST_LEN + 1, w) for w in POOL_WINDOWS]
    b = _pool_mix(sums, u, inv_cnt, w_pool_ref, pool_scale_ref[...])

    x1, hf = _mix_out(x, abuf[...], b, g_attn_ref, g_pool_ref, w_out_ref, ln_ffn_ref)
    ffn_out = None
    for c in range(D_FF // FF_CHUNK):
        ffn_out = _ffn_chunk(hf, c, ffn_out, w_up_ref, w_down_ref)
    y_ref[...] = _mlp_finish(x1 + ffn_out, p_ref[...], ln_ple_ref, w_gate_ref, w_ple_ref,
                             ln_final_ref)


def _resident(shape):
    zeros = (0,) * len(shape)
    return pl.BlockSpec(shape, lambda *_: zeros, pipeline_mode=pl.Buffered(1))


def _layer_weights(ln_mix, w_in, attn_sinks, g_attn_out, w_pool, pool_scale, g_pool_out, w_out,
                   ln_ffn, w_up, w_down, ln_ple, w_ple_gate, w_ple_proj, ln_final):
    perm = jnp.concatenate([jnp.arange(h * HEAD_DIM, (h + 1) * HEAD_DIM)
                            for h in _PAIR_HEAD_ORDER])
    w_in_p = jnp.concatenate([w_in[:, perm], w_in[:, Q_DIM:]], axis=1).astype(_BF16)
    w_out_p = jnp.concatenate([w_out[perm, :], w_out[Q_DIM:, :]], axis=0).astype(_BF16)
    row = lambda t: t.reshape(1, -1).astype(_F32)
    return dict(
        sinks=attn_sinks.astype(_F32),
        ln_mix=row(ln_mix), w_in=w_in_p, g_attn=row(g_attn_out[perm]), w_pool=w_pool.astype(_BF16),
        pool_scale=row(pool_scale), g_pool=row(g_pool_out), w_out=w_out_p, ln_ffn=row(ln_ffn),
        w_up=w_up.astype(_BF16), w_down=w_down.astype(_BF16), ln_ple=row(ln_ple),
        w_gate=w_ple_gate.astype(_BF16), w_ple=w_ple_proj.astype(_BF16), ln_final=row(ln_final))


_WEIGHT_ORDER = ("ln_mix", "w_in", "g_attn", "w_pool", "pool_scale", "g_pool", "w_out", "ln_ffn",
                 "w_up", "w_down", "ln_ple", "w_gate", "w_ple", "ln_final")


def _prompt_layer(x, p, lw):
    B, S, _ = x.shape
    T = SEQ_TILE
    assert S % T == 0 and T % (2 * CHUNK) == 0 and T >= WINDOW
    n_tiles = S // T
    n_total = B * n_tiles
    sink_row = jnp.repeat(lw["sinks"][jnp.array(_PAIR_HEAD_ORDER)], 2 * CHUNK).reshape(1, -1)
    weights = [lw[n] for n in _WEIGHT_ORDER]

    def mixer_tile(s):
        return jnp.minimum(s, n_total - 1)

    def mlp_tile(s):
        return jnp.maximum(s - 1, 0)

    def rows(width, which):
        return pl.BlockSpec((None, T, width),
                            lambda s: (which(s) // n_tiles, which(s) % n_tiles, 0))

    last = lambda nrows, width: pl.BlockSpec((None, nrows, width),
                                             lambda s: (mixer_tile(s) // n_tiles, 0, 0))
    return pl.pallas_call(
        functools.partial(_prompt_kernel, n_tiles=n_tiles, n_total=n_total),
        grid=(n_total + 1,),
        in_specs=[rows(D_MODEL, mixer_tile), rows(PLE_DIM, mlp_tile), _resident(sink_row.shape)]
                 + [_resident(w.shape) for w in weights],
        out_specs=[rows(D_MODEL, mlp_tile), last(WINDOW, KV_DIM), last(WINDOW, KV_DIM),
                   last(HIST_ROWS, POOL_WIDTH)],
        out_shape=[jax.ShapeDtypeStruct((B, S, D_MODEL), _F32),
                   jax.ShapeDtypeStruct((B, WINDOW, KV_DIM), _F32),
                   jax.ShapeDtypeStruct((B, WINDOW, KV_DIM), _F32),
                   jax.ShapeDtypeStruct((B, HIST_ROWS, POOL_WIDTH), _F32)],
        scratch_shapes=[pltpu.VMEM((WINDOW + T, KV_DIM), _BF16),
                        pltpu.VMEM((KV_DIM, WINDOW + T), _BF16),
                        pltpu.VMEM((HIST_ROWS + T, POOL_WIDTH), _F32),
                        pltpu.VMEM((Q_DIM, T), _F32),
                        pltpu.VMEM((T, D_MODEL), _F32),
                        pltpu.VMEM((T, D_MODEL), _BF16)],
        compiler_params=pltpu.CompilerParams(
            dimension_semantics=("arbitrary",), vmem_limit_bytes=VMEM_LIMIT_BYTES),
        name="prompt_layer",
    )(x, p, sink_row, *weights)


def _sample_layer(x, p, cache_k, cache_v, state_pool, lw):
    Bs, Ts, _ = x.shape
    R = Bs * Ts
    assert Ts == HIST_ROWS, "new frames per stream must fill one 16-row block"
    sink_col = jnp.repeat(lw["sinks"], Ts).reshape(N_HEADS * Ts, 1)
    weights = [lw[n] for n in _WEIGHT_ORDER]
    hist = jnp.pad(state_pool, ((0, 0), (HIST_ROWS - POOL_HIST, 0), (0, 0)))
    ins = [x.reshape(R, D_MODEL), p.reshape(R, PLE_DIM), cache_k.reshape(Bs, WINDOW, KV_DIM),
           cache_v.reshape(Bs, WINDOW, KV_DIM), hist, sink_col] + weights
    full = lambda shape: pl.BlockSpec(shape, lambda i: (0,) * len(shape))
    return pl.pallas_call(
        functools.partial(_sample_kernel, n_streams=Bs, n_frames=Ts),
        grid=(1,),
        in_specs=[_resident(a.shape) for a in ins],
        out_specs=[full((R, D_MODEL)), full((R, KV_DIM)), full((R, KV_DIM)), full((R, POOL_WIDTH))],
        out_shape=[jax.ShapeDtypeStruct((R, D_MODEL), _F32),
                   jax.ShapeDtypeStruct((R, KV_DIM), _F32),
                   jax.ShapeDtypeStruct((R, KV_DIM), _F32),
                   jax.ShapeDtypeStruct((R, POOL_WIDTH), _F32)],
        scratch_shapes=[pltpu.VMEM((R, Q_DIM), _BF16), pltpu.VMEM((R, KV_DIM), _BF16),
                        pltpu.VMEM((R, KV_DIM), _BF16), pltpu.VMEM((R, Q_DIM), _F32)],
        compiler_params=pltpu.CompilerParams(
            dimension_semantics=("arbitrary",), vmem_limit_bytes=VMEM_LIMIT_BYTES),
        name="sample_layer",
    )(*ins)


def kernel(x_prompt, x_sample, cache_k, cache_v, state_pool, p_prompt, p_sample, ln_mix, w_in,
           attn_sinks, g_attn_out, w_pool, pool_scale, g_pool_out, w_out, ln_ffn, w_up, w_down,
           ln_ple, w_ple_gate, w_ple_proj, ln_final):
    depth = ln_mix.shape[0]
    assert depth == 1, "the final norm is fused into the single layer's kernel"
    B = x_prompt.shape[0]
    Bs, Ts, _ = x_sample.shape
    lw = _layer_weights(ln_mix[0], w_in[0], attn_sinks[0], g_attn_out[0], w_pool[0], pool_scale[0],
                        g_pool_out[0], w_out[0], ln_ffn[0], w_up[0], w_down[0], ln_ple[0],
                        w_ple_gate[0], w_ple_proj[0], ln_final)
    y_p, k_p, v_p, pool_p = _prompt_layer(x_prompt, p_prompt[0], lw)
    y_s, k_s, v_s, u_s = _sample_layer(x_sample, p_sample[0], cache_k[0], cache_v[0],
                                       state_pool[0], lw)
    kv_p = lambda t: t.reshape(1, B, WINDOW, N_KV_HEADS, HEAD_DIM)
    kv_s = lambda t: t.reshape(1, Bs, Ts, N_KV_HEADS, HEAD_DIM)
    return (y_p, y_s.reshape(Bs, Ts, D_MODEL), kv_p(k_p), kv_p(v_p),
            pool_p[None, :, HIST_ROWS - POOL_HIST:, :],
            kv_s(k_s), kv_s(v_s),
            u_s.reshape(Bs, Ts, POOL_WIDTH)[None, :, Ts - POOL_HIST:, :])
```

```python
import functools

import jax
import jax.numpy as jnp
from jax import lax
from jax.experimental import pallas as pl
from jax.experimental.pallas import tpu as pltpu

D_MODEL = 1024
CHUNK = 64
WINDOW = 128
N_HEADS = 8
HEAD_DIM = 64
N_KV_HEADS = 2
GQA_GROUP = N_HEADS // N_KV_HEADS
Q_DIM = N_HEADS * HEAD_DIM
KV_DIM = N_KV_HEADS * HEAD_DIM
POOL_WIDTH = 512
POOL_WINDOWS = (2, 4, 8, 16)
POOL_GROUP_DIM = POOL_WIDTH // len(POOL_WINDOWS)
POOL_HIST = max(POOL_WINDOWS) - 1
HIST_ROWS = 16
IN_PROJ_DIM = Q_DIM + 2 * KV_DIM + POOL_WIDTH
D_FF = 4 * D_MODEL
PLE_DIM = 256
PAST_LEN = 2048
RMS_EPS = 1e-6
LANES = 128
HALF = LANES // 2

SEQ_TILE = 512
FF_CHUNK = 512
VMEM_LIMIT_BYTES = 58 * 1024 * 1024

_F32 = jnp.float32
_BF16 = jnp.bfloat16
_PAIR_HEAD_ORDER = tuple(h for p in range(GQA_GROUP) for h in (p, GQA_GROUP + p))


def _rms(x, g):
    ms = jnp.mean(x * x, axis=-1, keepdims=True)
    return x * lax.rsqrt(ms + RMS_EPS) * g


def _dot(a, w):
    return jnp.dot(a.astype(_BF16), w, preferred_element_type=_F32)


def _dot_nt(a, b):
    return lax.dot_general(a, b, (((1,), (1,)), ((), ())), preferred_element_type=_F32)


def _stack_heads(q_rows):
    lane = lax.broadcasted_iota(jnp.int32, (q_rows.shape[0], LANES), 1)
    lo, hi = [], []
    zero = jnp.zeros((q_rows.shape[0], LANES), q_rows.dtype)
    for p in range(GQA_GROUP):
        blk = q_rows[:, p * LANES:(p + 1) * LANES]
        lo.append(jnp.where(lane < HALF, blk, zero))
        hi.append(jnp.where(lane >= HALF, blk, zero))
    return jnp.concatenate(lo + hi, axis=0)


def _unstack_heads(o, rows):
    lane = lax.broadcasted_iota(jnp.int32, (rows, LANES), 1)
    blocks = []
    for p in range(GQA_GROUP):
        o_lo = o[p * rows:(p + 1) * rows]
        o_hi = o[(GQA_GROUP + p) * rows:(GQA_GROUP + p + 1) * rows]
        blocks.append(jnp.where(lane < HALF, o_lo, o_hi))
    return jnp.concatenate(blocks, axis=-1)


def _sink_softmax_pv(score_parts, value_parts, sink_col):
    m = sink_col
    for s in score_parts:
        m = jnp.maximum(m, jnp.max(s, axis=-1, keepdims=True))
    den = jnp.exp(sink_col - m)
    o = None
    for s, v in zip(score_parts, value_parts):
        e = jnp.exp(s - m)
        den = den + jnp.sum(e, axis=-1, keepdims=True)
        pv = jnp.dot(e.astype(_BF16), v, preferred_element_type=_F32)
        o = pv if o is None else o + pv
    return o * (1.0 / den)


def _attention_transposed(qT, kbuf, vTbuf, sink_row, first_tile, aT_ref, interleave=None):
    T = qT.shape[1]
    pair = 2 * CHUNK
    n_keys = WINDOW + pair
    n_cols = N_HEADS * pair
    zero = jnp.zeros((HEAD_DIM, pair), qT.dtype)
    lane = lax.broadcasted_iota(jnp.int32, (CHUNK, n_cols), 1) % pair
    neg = jnp.float32(-jnp.inf)
    for j in range(T // pair):
        cols = slice(j * pair, (j + 1) * pair)
        blocks = []
        for rb in range(N_HEADS):
            qh = qT[rb * HEAD_DIM:(rb + 1) * HEAD_DIM, cols]
            blocks.append(jnp.concatenate([qh, zero] if rb % 2 == 0 else [zero, qh], axis=0))
        rhs = jnp.concatenate(blocks, axis=1)
        sc = jnp.dot(kbuf[j * pair:j * pair + n_keys, :], rhs, preferred_element_type=_F32)
        s0 = jnp.where(lane >= CHUNK, neg, sc[0:CHUNK])
        s3 = jnp.where(lane < CHUNK, neg, sc[3 * CHUNK:4 * CHUNK])
        s1 = sc[CHUNK:2 * CHUNK]
        if j == 0:
            s0 = jnp.where(first_tile, neg, s0)
            s1 = jnp.where(first_tile, neg, s1)
        sc = jnp.concatenate([s0, s1, sc[2 * CHUNK:3 * CHUNK], s3], axis=0)
        m = jnp.maximum(jnp.max(sc, axis=0, keepdims=True), sink_row)
        if interleave is not None:
            interleave(2 * j)
        e = jnp.exp(sc - m)
        den = jnp.sum(e, axis=0, keepdims=True) + jnp.exp(sink_row - m)
        oT = jnp.dot(vTbuf[:, j * pair:j * pair + n_keys], e.astype(_BF16),
                     preferred_element_type=_F32)
        oT = oT * (1.0 / den)
        outs = []
        for rb in range(N_HEADS):
            g = rb % 2
            outs.append(oT[g * HEAD_DIM:(g + 1) * HEAD_DIM, rb * pair:(rb + 1) * pair])
        aT_ref[:, cols] = jnp.concatenate(outs, axis=0)
        if interleave is not None:
            interleave(2 * j + 1)


def _pool_windows(ext):
    outs = []
    for g, w in enumerate(POOL_WINDOWS):
        acc = ext[:, g * POOL_GROUP_DIM:(g + 1) * POOL_GROUP_DIM]
        span = 1
        while span < w:
            acc = acc + pltpu.roll(acc, span, 0)
            span *= 2
        outs.append(acc)
    return outs


def _pool_mix(sums, u, inv_cnt, w_pool_ref, pool_scale):
    outs = []
    for g in range(len(POOL_WINDOWS)):
        cols = slice(g * POOL_GROUP_DIM, (g + 1) * POOL_GROUP_DIM)
        d = sums[g] * inv_cnt[g] - u[:, cols]
        outs.append(_dot(d, w_pool_ref[g]))
    return jnp.concatenate(outs, axis=-1) * pool_scale


def _mix_out(x, a, b, g_attn_ref, g_pool_ref, w_out_ref, ln_ffn_ref):
    mix = jnp.concatenate([_rms(a, g_attn_ref[...]), _rms(b, g_pool_ref[...])], axis=-1)
    x = x + _dot(mix, w_out_ref[...])
    return x, _rms(x, ln_ffn_ref[...]).astype(_BF16)


def _ffn_chunk(hf, c, acc, w_up_ref, w_down_ref):
    cols = slice(c * FF_CHUNK, (c + 1) * FF_CHUNK)
    up = jnp.dot(hf, w_up_ref[:, cols], preferred_element_type=_F32)
    dn = _dot(jnp.square(jnp.maximum(up, 0.0)), w_down_ref[cols, :])
    return dn if acc is None else acc + dn


def _prompt_kernel(x_ref, p_ref, x1s_ref, hfs_ref, ps_ref, sink_ref, ln_mix_ref, w_in_ref,
                   g_attn_ref, w_pool_ref, pool_scale_ref, g_pool_ref, w_out_ref, ln_ffn_ref,
                   w_up_ref, w_down_ref, ln_ple_ref, w_gate_ref, w_ple_ref, ln_final_ref,
                   y_ref, ys_ref, k_out_ref, v_out_ref, pool_out_ref,
                   kbuf, vTbuf, ubuf, aT_buf, x1_carry, hf_carry, *, n_tiles, n_total):
    s = pl.program_id(0)
    i = lax.rem(jnp.minimum(s, n_total - 1), n_tiles)
    T = SEQ_TILE

    @pl.when(s == 0)
    def _():
        x1_carry[...] = x1s_ref[...]
        hf_carry[...] = hfs_ref[...]

    @pl.when(i == 0)
    def _():
        kbuf[0:WINDOW, :] = jnp.zeros((WINDOW, KV_DIM), _BF16)
        vTbuf[:, 0:WINDOW] = jnp.zeros((KV_DIM, WINDOW), _BF16)
        ubuf[0:HIST_ROWS, :] = jnp.zeros((HIST_ROWS, POOL_WIDTH), _F32)

    ffn_acc = [None]

    def ffn_chunk(c):
        ffn_acc[0] = _ffn_chunk(hf_carry[...], c, ffn_acc[0], w_up_ref, w_down_ref)

    assert D_FF // FF_CHUNK == 2 * (T // (2 * CHUNK))

    x = x_ref[...]
    z = _dot(_rms(x, ln_mix_ref[...]), w_in_ref[...])
    q = z[:, :Q_DIM] * (HEAD_DIM ** -0.5)
    k = z[:, Q_DIM:Q_DIM + KV_DIM]
    v = z[:, Q_DIM + KV_DIM:Q_DIM + 2 * KV_DIM]
    u = z[:, Q_DIM + 2 * KV_DIM:]
    kbuf[WINDOW:WINDOW + T, :] = k.astype(_BF16)
    vTbuf[:, WINDOW:WINDOW + T] = v.T.astype(_BF16)
    ubuf[HIST_ROWS:HIST_ROWS + T, :] = u

    _attention_transposed(q.T.astype(_BF16), kbuf, vTbuf, sink_ref[...], i == 0, aT_buf,
                          interleave=ffn_chunk)
    a = aT_buf[...].T

    sums = [sg[HIST_ROWS:] for sg in _pool_windows(ubuf[...])]
    pos1 = (i * T + 1 + lax.broadcasted_iota(jnp.int32, (T, 1), 0))
    inv_cnt = [1.0 / jnp.minimum(pos1, w).astype(_F32) for w in POOL_WINDOWS]
    b = _pool_mix(sums, u, inv_cnt, w_pool_ref, pool_scale_ref[...])
    mix = jnp.concatenate([_rms(a, g_attn_ref[...]), _rms(b, g_pool_ref[...])], axis=-1)

    x2 = x1_carry[...] + ffn_acc[0]
    n2 = _rms(x2, ln_ple_ref[...])
    x1 = x + _dot(mix, w_out_ref[...])
    gate_pre = _dot(n2, w_gate_ref[...])
    ple = _dot(jnp.where(s == 0, ps_ref[...], p_ref[...]), w_ple_ref[...])
    x1_carry[...] = x1
    hf_carry[...] = _rms(x1, ln_ffn_ref[...]).astype(_BF16)
    y_ref[...] = _rms(x2 + jax.nn.sigmoid(gate_pre) * ple, ln_final_ref[...])

    @pl.when(s == 0)
    def _():
        ys_ref[...] = y_ref[...]

    kbuf[0:WINDOW, :] = kbuf[T:T + WINDOW, :]
    vTbuf[:, 0:WINDOW] = vTbuf[:, T:T + WINDOW]
    ubuf[0:HIST_ROWS, :] = ubuf[T:T + HIST_ROWS, :]

    @pl.when(i == n_tiles - 1)
    def _():
        k_out_ref[...] = k[T - WINDOW:]
        v_out_ref[...] = v[T - WINDOW:]
        pool_out_ref[...] = u[T - HIST_ROWS:]


def _sample_kernel(x_ref, ck_ref, cv_ref, hist_ref, sink_ref, ln_mix_ref, w_in_ref, g_attn_ref,
                   w_pool_ref, pool_scale_ref, g_pool_ref, w_out_ref, ln_ffn_ref,
                   x1_ref, hf_ref, k_out_ref, v_out_ref, u_out_ref,
                   qbuf, kbuf, vbuf, abuf, *, n_streams, n_frames):
    x = x_ref[...]
    z = _dot(_rms(x, ln_mix_ref[...]), w_in_ref[...])
    k = z[:, Q_DIM:Q_DIM + KV_DIM]
    v = z[:, Q_DIM + KV_DIM:Q_DIM + 2 * KV_DIM]
    u = z[:, Q_DIM + 2 * KV_DIM:]
    qbuf[...] = (z[:, :Q_DIM] * (HEAD_DIM ** -0.5)).astype(_BF16)
    kbuf[...] = k.astype(_BF16)
    vbuf[...] = v.astype(_BF16)
    k_out_ref[...] = k
    v_out_ref[...] = v
    u_out_ref[...] = u

    sink_col = sink_ref[...]

    def stream_body(s_idx, carry):
        rows = pl.ds(pl.multiple_of(s_idx * n_frames, n_frames), n_frames)
        qs = _stack_heads(qbuf[rows, :])
        s_cache = _dot_nt(qs, ck_ref[s_idx].astype(_BF16))
        s_new = _dot_nt(qs, kbuf[rows, :])
        o = _sink_softmax_pv([s_cache, s_new], [cv_ref[s_idx].astype(_BF16), vbuf[rows, :]],
                             sink_col)
        abuf[rows, :] = _unstack_heads(o, n_frames)
        return carry

    lax.fori_loop(0, n_streams, stream_body, 0)

    ext = jnp.concatenate([hist_ref[...], u.reshape(n_streams, n_frames, POOL_WIDTH)], axis=1)
    ext = ext.reshape(n_streams * (HIST_ROWS + n_frames), POOL_WIDTH)
    sums = [sg.reshape(n_streams, HIST_ROWS + n_frames, POOL_GROUP_DIM)[:, HIST_ROWS:, :]
            .reshape(n_streams * n_frames, POOL_GROUP_DIM) for sg in _pool_windows(ext)]
    inv_cnt = [1.0 / min(PAST_LEN + 1, w) for w in POOL_WINDOWS]
    b = _pool_mix(sums, u, inv_cnt, w_pool_ref, pool_scale_ref[...])

    x1, hf = _mix_out(x, abuf[...], b, g_attn_ref, g_pool_ref, w_out_ref, ln_ffn_ref)
    x1_ref[...] = x1
    hf_ref[...] = hf


def _resident(shape):
    zeros = (0,) * len(shape)
    return pl.BlockSpec(shape, lambda *_: zeros, pipeline_mode=pl.Buffered(1))


def _pair_order(t, axis):
    shape = t.shape
    t = t.reshape(shape[:axis] + (N_KV_HEADS, GQA_GROUP, HEAD_DIM) + shape[axis + 1:])
    return jnp.swapaxes(t, axis, axis + 1).reshape(shape)


def _layer_weights(ln_mix, w_in, attn_sinks, g_attn_out, w_pool, pool_scale, g_pool_out, w_out,
                   ln_ffn, w_up, w_down, ln_ple, w_ple_gate, w_ple_proj, ln_final):
    w_in_p = jnp.concatenate([_pair_order(w_in[:, :Q_DIM], 1), w_in[:, Q_DIM:]],
                             axis=1).astype(_BF16)
    w_out_p = jnp.concatenate([_pair_order(w_out[:Q_DIM], 0), w_out[Q_DIM:]],
                              axis=0).astype(_BF16)
    row = lambda t: t.reshape(1, -1).astype(_F32)
    return dict(
        sinks=attn_sinks.astype(_F32),
        ln_mix=row(ln_mix), w_in=w_in_p, g_attn=row(_pair_order(g_attn_out, 0)),
        w_pool=w_pool.astype(_BF16), pool_scale=row(pool_scale), g_pool=row(g_pool_out),
        w_out=w_out_p, ln_ffn=row(ln_ffn), w_up=w_up.astype(_BF16), w_down=w_down.astype(_BF16),
        ln_ple=row(ln_ple), w_gate=w_ple_gate.astype(_BF16), w_ple=w_ple_proj.astype(_BF16),
        ln_final=row(ln_final))


_MIXER_WEIGHTS = ("ln_mix", "w_in", "g_attn", "w_pool", "pool_scale", "g_pool", "w_out", "ln_ffn")
_MLP_WEIGHTS = ("w_up", "w_down", "ln_ple", "w_gate", "w_ple", "ln_final")


def _prompt_layer(x, p, x1_s, hf_s, p_s, lw):
    B, S, _ = x.shape
    T = SEQ_TILE
    assert S % T == 0 and T % (2 * CHUNK) == 0 and T >= WINDOW
    assert x1_s.shape == (T, D_MODEL), "the sample rows must fill exactly one tile"
    n_tiles = S // T
    n_total = B * n_tiles
    sink_row = jnp.repeat(lw["sinks"][jnp.array(_PAIR_HEAD_ORDER)], 2 * CHUNK).reshape(1, -1)
    weights = [lw[n] for n in _MIXER_WEIGHTS + _MLP_WEIGHTS]
    sample = [x1_s, hf_s, p_s]

    def mixer_tile(s):
        return jnp.minimum(s, n_total - 1)

    def mlp_tile(s):
        return jnp.maximum(s - 1, 0)

    def rows(width, which):
        return pl.BlockSpec((None, T, width),
                            lambda s: (which(s) // n_tiles, which(s) % n_tiles, 0))

    last = lambda nrows, width: pl.BlockSpec((None, nrows, width),
                                             lambda s: (mixer_tile(s) // n_tiles, 0, 0))
    return pl.pallas_call(
        functools.partial(_prompt_kernel, n_tiles=n_tiles, n_total=n_total),
        grid=(n_total + 1,),
        in_specs=[rows(D_MODEL, mixer_tile), rows(PLE_DIM, mlp_tile)]
                 + [_resident(a.shape) for a in sample + [sink_row] + weights],
        out_specs=[rows(D_MODEL, mlp_tile), pl.BlockSpec((T, D_MODEL), lambda s: (0, 0)),
                   last(WINDOW, KV_DIM), last(WINDOW, KV_DIM), last(HIST_ROWS, POOL_WIDTH)],
        out_shape=[jax.ShapeDtypeStruct((B, S, D_MODEL), _F32),
                   jax.ShapeDtypeStruct((T, D_MODEL), _F32),
                   jax.ShapeDtypeStruct((B, WINDOW, KV_DIM), _F32),
                   jax.ShapeDtypeStruct((B, WINDOW, KV_DIM), _F32),
                   jax.ShapeDtypeStruct((B, HIST_ROWS, POOL_WIDTH), _F32)],
        scratch_shapes=[pltpu.VMEM((WINDOW + T, KV_DIM), _BF16),
                        pltpu.VMEM((KV_DIM, WINDOW + T), _BF16),
                        pltpu.VMEM((HIST_ROWS + T, POOL_WIDTH), _F32),
                        pltpu.VMEM((Q_DIM, T), _F32),
                        pltpu.VMEM((T, D_MODEL), _F32),
                        pltpu.VMEM((T, D_MODEL), _BF16)],
        compiler_params=pltpu.CompilerParams(
            dimension_semantics=("arbitrary",), vmem_limit_bytes=VMEM_LIMIT_BYTES),
        name="prompt_layer",
    )(x, p, *sample, sink_row, *weights)


def _sample_layer(x, cache_k, cache_v, state_pool, lw):
    Bs, Ts, _ = x.shape
    R = Bs * Ts
    assert Ts == HIST_ROWS, "new frames per stream must fill one 16-row block"
    sink_col = jnp.repeat(lw["sinks"], Ts).reshape(N_HEADS * Ts, 1)
    weights = [lw[n] for n in _MIXER_WEIGHTS]
    hist = jnp.pad(state_pool, ((0, 0), (HIST_ROWS - POOL_HIST, 0), (0, 0)))
    ins = [x.reshape(R, D_MODEL), cache_k.reshape(Bs, WINDOW, KV_DIM),
           cache_v.reshape(Bs, WINDOW, KV_DIM), hist, sink_col] + weights
    outs = [((R, D_MODEL), _F32), ((R, D_MODEL), _BF16), ((R, KV_DIM), _F32), ((R, KV_DIM), _F32),
            ((R, POOL_WIDTH), _F32)]
    return pl.pallas_call(
        functools.partial(_sample_kernel, n_streams=Bs, n_frames=Ts),
        grid=(1,),
        in_specs=[_resident(a.shape) for a in ins],
        out_specs=[pl.BlockSpec(shape, lambda i: (0, 0)) for shape, _ in outs],
        out_shape=[jax.ShapeDtypeStruct(shape, dtype) for shape, dtype in outs],
        scratch_shapes=[pltpu.VMEM((R, Q_DIM), _BF16), pltpu.VMEM((R, KV_DIM), _BF16),
                        pltpu.VMEM((R, KV_DIM), _BF16), pltpu.VMEM((R, Q_DIM), _F32)],
        compiler_params=pltpu.CompilerParams(
            dimension_semantics=("arbitrary",), vmem_limit_bytes=VMEM_LIMIT_BYTES),
        name="sample_mixer",
    )(*ins)


def kernel(x_prompt, x_sample, cache_k, cache_v, state_pool, p_prompt, p_sample, ln_mix, w_in,
           attn_sinks, g_attn_out, w_pool, pool_scale, g_pool_out, w_out, ln_ffn, w_up, w_down,
           ln_ple, w_ple_gate, w_ple_proj, ln_final):
    depth = ln_mix.shape[0]
    assert depth == 1, "the final norm is fused into the single layer's kernel"
    B = x_prompt.shape[0]
    Bs, Ts, _ = x_sample.shape
    lw = _layer_weights(ln_mix[0], w_in[0], attn_sinks[0], g_attn_out[0], w_pool[0], pool_scale[0],
                        g_pool_out[0], w_out[0], ln_ffn[0], w_up[0], w_down[0], ln_ple[0],
                        w_ple_gate[0], w_ple_proj[0], ln_final)
    x1_s, hf_s, k_s, v_s, u_s = _sample_layer(x_sample, cache_k[0], cache_v[0], state_pool[0], lw)
    y_p, y_s, k_p, v_p, pool_p = _prompt_layer(x_prompt, p_prompt[0], x1_s, hf_s,
                                               p_sample[0].reshape(Bs * Ts, PLE_DIM), lw)
    kv_p = lambda t: t.reshape(1, B, WINDOW, N_KV_HEADS, HEAD_DIM)
    kv_s = lambda t: t.reshape(1, Bs, Ts, N_KV_HEADS, HEAD_DIM)
    return (y_p, y_s.reshape(Bs, Ts, D_MODEL), kv_p(k_p), kv_p(v_p),
            pool_p[None, :, HIST_ROWS - POOL_HIST:, :],
            kv_s(k_s), kv_s(v_s),
            u_s.reshape(Bs, Ts, POOL_WIDTH)[None, :, Ts - POOL_HIST:, :])
```

```python
import functools

import jax
import jax.numpy as jnp
from jax import lax
from jax.experimental import pallas as pl
from jax.experimental.pallas import tpu as pltpu

D_MODEL = 1024
CHUNK = 64
WINDOW = 128
N_HEADS = 8
HEAD_DIM = 64
N_KV_HEADS = 2
GQA_GROUP = N_HEADS // N_KV_HEADS
Q_DIM = N_HEADS * HEAD_DIM
KV_DIM = N_KV_HEADS * HEAD_DIM
POOL_WIDTH = 512
POOL_WINDOWS = (2, 4, 8, 16)
POOL_GROUP_DIM = POOL_WIDTH // len(POOL_WINDOWS)
POOL_HIST = max(POOL_WINDOWS) - 1
HIST_ROWS = 16
IN_PROJ_DIM = Q_DIM + 2 * KV_DIM + POOL_WIDTH
D_FF = 4 * D_MODEL
PLE_DIM = 256
PAST_LEN = 2048
RMS_EPS = 1e-6
LANES = 128
HALF = LANES // 2

SEQ_TILE = 512
FF_CHUNK = 512
SAMPLE_STEPS = 8
VMEM_LIMIT_BYTES = 58 * 1024 * 1024

_F32 = jnp.float32
_BF16 = jnp.bfloat16
_VEC_ROWS = {"ln_mix": (0, D_MODEL), "g_attn": (1, Q_DIM), "pool_scale": (2, POOL_WIDTH),
             "g_pool": (3, POOL_WIDTH), "ln_ffn": (4, D_MODEL), "ln_ple": (5, D_MODEL),
             "ln_final": (6, D_MODEL), "sink_row": (7, N_HEADS * 2 * CHUNK)}
_PAIR_HEAD_ORDER = tuple(h for p in range(GQA_GROUP) for h in (p, GQA_GROUP + p))


def _rms(x, g):
    ms = jnp.mean(x * x, axis=-1, keepdims=True)
    return x * lax.rsqrt(ms + RMS_EPS) * g


def _dot(a, w):
    return jnp.dot(a.astype(_BF16), w, preferred_element_type=_F32)


def _dot_nt(a, b):
    return lax.dot_general(a, b, (((1,), (1,)), ((), ())), preferred_element_type=_F32)


def _stack_heads(q_rows):
    lane = lax.broadcasted_iota(jnp.int32, (q_rows.shape[0], LANES), 1)
    lo, hi = [], []
    zero = jnp.zeros((q_rows.shape[0], LANES), q_rows.dtype)
    for p in range(GQA_GROUP):
        blk = q_rows[:, p * LANES:(p + 1) * LANES]
        lo.append(jnp.where(lane < HALF, blk, zero))
        hi.append(jnp.where(lane >= HALF, blk, zero))
    return jnp.concatenate(lo + hi, axis=0)


def _unstack_heads(o, rows):
    lane = lax.broadcasted_iota(jnp.int32, (rows, LANES), 1)
    blocks = []
    for p in range(GQA_GROUP):
        o_lo = o[p * rows:(p + 1) * rows]
        o_hi = o[(GQA_GROUP + p) * rows:(GQA_GROUP + p + 1) * rows]
        blocks.append(jnp.where(lane < HALF, o_lo, o_hi))
    return jnp.concatenate(blocks, axis=-1)


def _sink_softmax_pv(score_parts, value_parts, sink_col):
    m = sink_col
    for s in score_parts:
        m = jnp.maximum(m, jnp.max(s, axis=-1, keepdims=True))
    den = jnp.exp(sink_col - m)
    o = None
    for s, v in zip(score_parts, value_parts):
        e = jnp.exp(s - m)
        den = den + jnp.sum(e, axis=-1, keepdims=True)
        pv = jnp.dot(e.astype(_BF16), v, preferred_element_type=_F32)
        o = pv if o is None else o + pv
    return o * (1.0 / den)


def _attention_transposed(qT, kbuf, vTbuf, sink_row, first_tile, aT_ref, interleave=None):
    T = qT.shape[1]
    pair = 2 * CHUNK
    n_keys = WINDOW + pair
    n_cols = N_HEADS * pair
    zero = jnp.zeros((HEAD_DIM, pair), qT.dtype)
    lane = lax.broadcasted_iota(jnp.int32, (CHUNK, n_cols), 1) % pair
    neg = jnp.float32(-jnp.inf)
    for j in range(T // pair):
        cols = slice(j * pair, (j + 1) * pair)
        blocks = []
        for rb in range(N_HEADS):
            qh = qT[rb * HEAD_DIM:(rb + 1) * HEAD_DIM, cols]
            blocks.append(jnp.concatenate([qh, zero] if rb % 2 == 0 else [zero, qh], axis=0))
        rhs = jnp.concatenate(blocks, axis=1)
        sc = jnp.dot(kbuf[j * pair:j * pair + n_keys, :], rhs, preferred_element_type=_F32)
        s0 = jnp.where(lane >= CHUNK, neg, sc[0:CHUNK])
        s3 = jnp.where(lane < CHUNK, neg, sc[3 * CHUNK:4 * CHUNK])
        s1 = sc[CHUNK:2 * CHUNK]
        if j == 0:
            s0 = jnp.where(first_tile, neg, s0)
            s1 = jnp.where(first_tile, neg, s1)
        sc = jnp.concatenate([s0, s1, sc[2 * CHUNK:3 * CHUNK], s3], axis=0)
        m = jnp.maximum(jnp.max(sc, axis=0, keepdims=True), sink_row)
        if interleave is not None:
            interleave(2 * j)
        e = jnp.exp(sc - m)
        den = jnp.sum(e, axis=0, keepdims=True) + jnp.exp(sink_row - m)
        oT = jnp.dot(vTbuf[:, j * pair:j * pair + n_keys], e.astype(_BF16),
                     preferred_element_type=_F32)
        oT = oT * (1.0 / den)
        outs = []
        for rb in range(N_HEADS):
            g = rb % 2
            outs.append(oT[g * HEAD_DIM:(g + 1) * HEAD_DIM, rb * pair:(rb + 1) * pair])
        aT_ref[:, cols] = jnp.concatenate(outs, axis=0)
        if interleave is not None:
            interleave(2 * j + 1)


def _pool_windows(ext):
    outs = []
    for g, w in enumerate(POOL_WINDOWS):
        acc = ext[:, g * POOL_GROUP_DIM:(g + 1) * POOL_GROUP_DIM]
        span = 1
        while span < w:
            acc = acc + pltpu.roll(acc, span, 0)
            span *= 2
        outs.append(acc)
    return outs


def _pool_mix(sums, u, inv_cnt, w_pool_ref, pool_scale):
    outs = []
    for g in range(len(POOL_WINDOWS)):
        cols = slice(g * POOL_GROUP_DIM, (g + 1) * POOL_GROUP_DIM)
        d = sums[g] * inv_cnt[g] - u[:, cols]
        outs.append(_dot(d, w_pool_ref[g]))
    return jnp.concatenate(outs, axis=-1) * pool_scale


def _mix_out(x, a, b, g_attn, g_pool, w_out_ref, ln_ffn):
    mix = jnp.concatenate([_rms(a, g_attn), _rms(b, g_pool)], axis=-1)
    x = x + _dot(mix, w_out_ref[...])
    return x, _rms(x, ln_ffn).astype(_BF16)


def _vec(vec_ref, name):
    row, width = _VEC_ROWS[name]
    return vec_ref[row:row + 1, :width]


def _ffn_chunk(hf, c, acc, w_up_ref, w_down_ref):
    cols = slice(c * FF_CHUNK, (c + 1) * FF_CHUNK)
    up = jnp.dot(hf, w_up_ref[:, cols], preferred_element_type=_F32)
    dn = _dot(jnp.square(jnp.maximum(up, 0.0)), w_down_ref[cols, :])
    return dn if acc is None else acc + dn


def _prompt_kernel(x_ref, p_ref, x1s_ref, hfs_ref, ps_ref, vec_ref, w_in_ref, w_pool_ref, w_out_ref,
                   w_up_ref, w_down_ref, w_gate_ref, w_ple_ref,
                   y_ref, ys_ref, k_out_ref, v_out_ref, pool_out_ref,
                   kbuf, vTbuf, ubuf, aT_buf, x1_carry, hf_carry, *, n_tiles, n_total):
    s = pl.program_id(0)
    i = lax.rem(jnp.minimum(s, n_total - 1), n_tiles)
    T = SEQ_TILE

    @pl.when(s == 0)
    def _():
        x1_carry[...] = x1s_ref[...]
        hf_carry[...] = hfs_ref[...]

    @pl.when(i == 0)
    def _():
        kbuf[0:WINDOW, :] = jnp.zeros((WINDOW, KV_DIM), _BF16)
        vTbuf[:, 0:WINDOW] = jnp.zeros((KV_DIM, WINDOW), _BF16)
        ubuf[0:HIST_ROWS, :] = jnp.zeros((HIST_ROWS, POOL_WIDTH), _F32)

    @pl.when(s < n_total)
    def _both_halves():
        ffn_acc = [None]

        def ffn_chunk(c):
            ffn_acc[0] = _ffn_chunk(hf_carry[...], c, ffn_acc[0], w_up_ref, w_down_ref)

        assert D_FF // FF_CHUNK == 2 * (T // (2 * CHUNK))

        x = x_ref[...]
        z = _dot(_rms(x, _vec(vec_ref, "ln_mix")), w_in_ref[...])
        q = z[:, :Q_DIM] * (HEAD_DIM ** -0.5)
        k = z[:, Q_DIM:Q_DIM + KV_DIM]
        v = z[:, Q_DIM + KV_DIM:Q_DIM + 2 * KV_DIM]
        u = z[:, Q_DIM + 2 * KV_DIM:]
        kbuf[WINDOW:WINDOW + T, :] = k.astype(_BF16)
        vTbuf[:, WINDOW:WINDOW + T] = v.T.astype(_BF16)
        ubuf[HIST_ROWS:HIST_ROWS + T, :] = u

        _attention_transposed(q.T.astype(_BF16), kbuf, vTbuf, _vec(vec_ref, "sink_row"), i == 0, aT_buf,
                              interleave=ffn_chunk)
        a = aT_buf[...].T

        sums = [sg[HIST_ROWS:] for sg in _pool_windows(ubuf[...])]
        pos1 = (i * T + 1 + lax.broadcasted_iota(jnp.int32, (T, 1), 0))
        inv_cnt = [1.0 / jnp.minimum(pos1, w).astype(_F32) for w in POOL_WINDOWS]
        b = _pool_mix(sums, u, inv_cnt, w_pool_ref, _vec(vec_ref, "pool_scale"))
        mix = jnp.concatenate([_rms(a, _vec(vec_ref, "g_attn")), _rms(b, _vec(vec_ref, "g_pool"))],
                              axis=-1)

        x2 = x1_carry[...] + ffn_acc[0]
        n2 = _rms(x2, _vec(vec_ref, "ln_ple"))
        x1 = x + _dot(mix, w_out_ref[...])
        gate_pre = _dot(n2, w_gate_ref[...])
        ple = _dot(jnp.where(s == 0, ps_ref[...], p_ref[...]), w_ple_ref[...])
        x1_carry[...] = x1
        hf_carry[...] = _rms(x1, _vec(vec_ref, "ln_ffn")).astype(_BF16)
        y_ref[...] = _rms(x2 + jax.nn.sigmoid(gate_pre) * ple, _vec(vec_ref, "ln_final"))

        @pl.when(s == 0)
        def _():
            ys_ref[...] = y_ref[...]

        kbuf[0:WINDOW, :] = kbuf[T:T + WINDOW, :]
        vTbuf[:, 0:WINDOW] = vTbuf[:, T:T + WINDOW]
        ubuf[0:HIST_ROWS, :] = ubuf[T:T + HIST_ROWS, :]

        @pl.when(i == n_tiles - 1)
        def _():
            k_out_ref[...] = k[T - WINDOW:]
            v_out_ref[...] = v[T - WINDOW:]
            pool_out_ref[...] = u[T - HIST_ROWS:]

    @pl.when(s == n_total)
    def _last_mlp_half():
        acc = None
        for c in range(D_FF // FF_CHUNK):
            acc = _ffn_chunk(hf_carry[...], c, acc, w_up_ref, w_down_ref)
        x2 = x1_carry[...] + acc
        gate_pre = _dot(_rms(x2, _vec(vec_ref, "ln_ple")), w_gate_ref[...])
        ple = _dot(p_ref[...], w_ple_ref[...])
        y_ref[...] = _rms(x2 + jax.nn.sigmoid(gate_pre) * ple, _vec(vec_ref, "ln_final"))


def _sample_kernel(x_ref, ck_ref, cv_ref, hist_ref, sink_ref, vec_ref, w_in_ref, w_pool_ref,
                   w_out_ref, w_up_ref, w_down_ref, w_gate_ref, w_ple_ref,
                   x1_ref, hf_ref, k_out_ref, v_out_ref, u_out_ref,
                   w_up_bf_ref, w_down_bf_ref, w_gate_bf_ref, w_ple_bf_ref,
                   qbuf, kbuf, vbuf, abuf, *, n_streams, n_frames, n_steps):
    g = pl.program_id(0)

    w_up_bf_ref[...] = w_up_ref[...].astype(_BF16)
    w_down_bf_ref[...] = w_down_ref[...].astype(_BF16)
    w_gate_bf_ref[...] = w_gate_ref[...].astype(_BF16)
    w_ple_bf_ref[...] = w_ple_ref[...].astype(_BF16)

    @pl.when(g == 0)
    def _():
        z = _dot(_rms(x_ref[...], _vec(vec_ref, "ln_mix")), w_in_ref[...])
        k = z[:, Q_DIM:Q_DIM + KV_DIM]
        v = z[:, Q_DIM + KV_DIM:Q_DIM + 2 * KV_DIM]
        qbuf[...] = (z[:, :Q_DIM] * (HEAD_DIM ** -0.5)).astype(_BF16)
        kbuf[...] = k.astype(_BF16)
        vbuf[...] = v.astype(_BF16)
        k_out_ref[...] = k
        v_out_ref[...] = v
        u_out_ref[...] = z[:, Q_DIM + 2 * KV_DIM:]

    sink_col = sink_ref[...]
    streams_per_step = n_streams // n_steps
    for t in range(streams_per_step):
        s_idx = g * streams_per_step + t
        rows = pl.ds(pl.multiple_of(s_idx * n_frames, n_frames), n_frames)
        qs = _stack_heads(qbuf[rows, :])
        s_cache = _dot_nt(qs, ck_ref[s_idx].astype(_BF16))
        s_new = _dot_nt(qs, kbuf[rows, :])
        o = _sink_softmax_pv([s_cache, s_new], [cv_ref[s_idx].astype(_BF16), vbuf[rows, :]],
                             sink_col)
        abuf[rows, :] = _unstack_heads(o, n_frames)

    @pl.when(g == n_steps - 1)
    def _():
        u = u_out_ref[...]
        ext = jnp.concatenate([hist_ref[...], u.reshape(n_streams, n_frames, POOL_WIDTH)], axis=1)
        ext = ext.reshape(n_streams * (HIST_ROWS + n_frames), POOL_WIDTH)
        sums = [sg.reshape(n_streams, HIST_ROWS + n_frames, POOL_GROUP_DIM)[:, HIST_ROWS:, :]
                .reshape(n_streams * n_frames, POOL_GROUP_DIM) for sg in _pool_windows(ext)]
        inv_cnt = [1.0 / min(PAST_LEN + 1, w) for w in POOL_WINDOWS]
        b = _pool_mix(sums, u, inv_cnt, w_pool_ref, _vec(vec_ref, "pool_scale"))

        x1, hf = _mix_out(x_ref[...], abuf[...], b, _vec(vec_ref, "g_attn"), _vec(vec_ref, "g_pool"),
                          w_out_ref, _vec(vec_ref, "ln_ffn"))
        x1_ref[...] = x1
        hf_ref[...] = hf


def _resident(shape):
    zeros = (0,) * len(shape)
    return pl.BlockSpec(shape, lambda *_: zeros, pipeline_mode=pl.Buffered(1))


def _pair_order(t, axis):
    shape = t.shape
    t = t.reshape(shape[:axis] + (N_KV_HEADS, GQA_GROUP, HEAD_DIM) + shape[axis + 1:])
    return jnp.swapaxes(t, axis, axis + 1).reshape(shape)


def _layer_weights(ln_mix, w_in, attn_sinks, g_attn_out, w_pool, pool_scale, g_pool_out, w_out,
                   ln_ffn, w_up, w_down, ln_ple, w_ple_gate, w_ple_proj, ln_final):
    w_in_p = jnp.concatenate([_pair_order(w_in[:, :Q_DIM], 1), w_in[:, Q_DIM:]],
                             axis=1).astype(_BF16)
    w_out_p = jnp.concatenate([_pair_order(w_out[:Q_DIM], 0), w_out[Q_DIM:]],
                              axis=0).astype(_BF16)
    vec_values = {"ln_mix": ln_mix, "g_attn": _pair_order(g_attn_out, 0), "pool_scale": pool_scale,
                  "g_pool": g_pool_out, "ln_ffn": ln_ffn, "ln_ple": ln_ple, "ln_final": ln_final,
                  "sink_row": jnp.repeat(attn_sinks[jnp.array(_PAIR_HEAD_ORDER)], 2 * CHUNK)}
    rows = [None] * len(_VEC_ROWS)
    for name, (r, width) in _VEC_ROWS.items():
        rows[r] = jnp.pad(vec_values[name].astype(_F32), (0, D_MODEL - width))
    return dict(
        sinks=attn_sinks.astype(_F32), vecs=jnp.stack(rows), w_in=w_in_p,
        w_pool=w_pool.astype(_BF16), w_out=w_out_p, w_up=w_up, w_down=w_down, w_gate=w_ple_gate,
        w_ple=w_ple_proj)


_MIXER_WEIGHTS = ("vecs", "w_in", "w_pool", "w_out")
_MLP_WEIGHTS = ("w_up", "w_down", "w_gate", "w_ple")


def _prompt_layer(x, p, x1_s, hf_s, p_s, lw, mlp_weights):
    B, S, _ = x.shape
    T = SEQ_TILE
    assert S % T == 0 and T % (2 * CHUNK) == 0 and T >= WINDOW
    assert x1_s.shape == (T, D_MODEL), "the sample rows must fill exactly one tile"
    n_tiles = S // T
    n_total = B * n_tiles
    weights = [lw[n] for n in _MIXER_WEIGHTS] + list(mlp_weights)
    sample = [x1_s, hf_s, p_s]

    def mixer_tile(s):
        return jnp.minimum(s, n_total - 1)

    def mlp_tile(s):
        return jnp.maximum(s - 1, 0)

    def rows(width, which):
        return pl.BlockSpec((None, T, width),
                            lambda s: (which(s) // n_tiles, which(s) % n_tiles, 0))

    last = lambda nrows, width: pl.BlockSpec((None, nrows, width),
                                             lambda s: (mixer_tile(s) // n_tiles, 0, 0))
    return pl.pallas_call(
        functools.partial(_prompt_kernel, n_tiles=n_tiles, n_total=n_total),
        grid=(n_total + 1,),
        in_specs=[rows(D_MODEL, mixer_tile), rows(PLE_DIM, mlp_tile)]
                 + [_resident(a.shape) for a in sample + weights],
        out_specs=[rows(D_MODEL, mlp_tile), pl.BlockSpec((T, D_MODEL), lambda s: (0, 0)),
                   last(WINDOW, KV_DIM), last(WINDOW, KV_DIM), last(HIST_ROWS, POOL_WIDTH)],
        out_shape=[jax.ShapeDtypeStruct((B, S, D_MODEL), _F32),
                   jax.ShapeDtypeStruct((T, D_MODEL), _F32),
                   jax.ShapeDtypeStruct((B, WINDOW, KV_DIM), _F32),
                   jax.ShapeDtypeStruct((B, WINDOW, KV_DIM), _F32),
                   jax.ShapeDtypeStruct((B, HIST_ROWS, POOL_WIDTH), _F32)],
        scratch_shapes=[pltpu.VMEM((WINDOW + T, KV_DIM), _BF16),
                        pltpu.VMEM((KV_DIM, WINDOW + T), _BF16),
                        pltpu.VMEM((HIST_ROWS + T, POOL_WIDTH), _F32),
                        pltpu.VMEM((Q_DIM, T), _F32),
                        pltpu.VMEM((T, D_MODEL), _F32),
                        pltpu.VMEM((T, D_MODEL), _BF16)],
        compiler_params=pltpu.CompilerParams(
            dimension_semantics=("arbitrary",), vmem_limit_bytes=VMEM_LIMIT_BYTES),
        name="prompt_layer",
    )(x, p, *sample, *weights)


def _sample_layer(x, cache_k, cache_v, state_pool, lw):
    Bs, Ts, _ = x.shape
    R = Bs * Ts
    n_steps = SAMPLE_STEPS
    assert Ts == HIST_ROWS, "new frames per stream must fill one 16-row block"
    assert Bs % n_steps == 0
    sink_col = jnp.repeat(lw["sinks"], Ts).reshape(N_HEADS * Ts, 1)
    hist = jnp.pad(state_pool, ((0, 0), (HIST_ROWS - POOL_HIST, 0), (0, 0)))
    resident = [x.reshape(R, D_MODEL), cache_k.reshape(Bs, WINDOW, KV_DIM),
                cache_v.reshape(Bs, WINDOW, KV_DIM), hist, sink_col] + [lw[n] for n in _MIXER_WEIGHTS]
    mlp_weights = [lw[n] for n in _MLP_WEIGHTS]
    row_block = lambda w: pl.BlockSpec((w.shape[0] // n_steps, w.shape[1]), lambda g: (g, 0))
    outs = [((R, D_MODEL), _F32), ((R, D_MODEL), _BF16), ((R, KV_DIM), _F32), ((R, KV_DIM), _F32),
            ((R, POOL_WIDTH), _F32)]
    res = pl.pallas_call(
        functools.partial(_sample_kernel, n_streams=Bs, n_frames=Ts, n_steps=n_steps),
        grid=(n_steps,),
        in_specs=[_resident(a.shape) for a in resident] + [row_block(w) for w in mlp_weights],
        out_specs=[pl.BlockSpec(shape, lambda g: (0, 0)) for shape, _ in outs]
                  + [row_block(w) for w in mlp_weights],
        out_shape=[jax.ShapeDtypeStruct(shape, dtype) for shape, dtype in outs]
                  + [jax.ShapeDtypeStruct(w.shape, _BF16) for w in mlp_weights],
        scratch_shapes=[pltpu.VMEM((R, Q_DIM), _BF16), pltpu.VMEM((R, KV_DIM), _BF16),
                        pltpu.VMEM((R, KV_DIM), _BF16), pltpu.VMEM((R, Q_DIM), _F32)],
        compiler_params=pltpu.CompilerParams(
            dimension_semantics=("arbitrary",), vmem_limit_bytes=VMEM_LIMIT_BYTES),
        name="sample_mixer",
    )(*resident, *mlp_weights)
    return res[:len(outs)], res[len(outs):]


def kernel(x_prompt, x_sample, cache_k, cache_v, state_pool, p_prompt, p_sample, ln_mix, w_in,
           attn_sinks, g_attn_out, w_pool, pool_scale, g_pool_out, w_out, ln_ffn, w_up, w_down,
           ln_ple, w_ple_gate, w_ple_proj, ln_final):
    depth = ln_mix.shape[0]
    assert depth == 1, "the final norm is fused into the single layer's kernel"
    B = x_prompt.shape[0]
    Bs, Ts, _ = x_sample.shape
    lw = _layer_weights(ln_mix[0], w_in[0], attn_sinks[0], g_attn_out[0], w_pool[0], pool_scale[0],
                        g_pool_out[0], w_out[0], ln_ffn[0], w_up[0], w_down[0], ln_ple[0],
                        w_ple_gate[0], w_ple_proj[0], ln_final)
    (x1_s, hf_s, k_s, v_s, u_s), mlp_weights = _sample_layer(x_sample, cache_k[0], cache_v[0],
                                                             state_pool[0], lw)
    y_p, y_s, k_p, v_p, pool_p = _prompt_layer(x_prompt, p_prompt[0], x1_s, hf_s,
                                               p_sample[0].reshape(Bs * Ts, PLE_DIM), lw, mlp_weights)
    kv_p = lambda t: t.reshape(1, B, WINDOW, N_KV_HEADS, HEAD_DIM)
    kv_s = lambda t: t.reshape(1, Bs, Ts, N_KV_HEADS, HEAD_DIM)
    return (y_p, y_s.reshape(Bs, Ts, D_MODEL), kv_p(k_p), kv_p(v_p),
            pool_p[None, :, HIST_ROWS - POOL_HIST:, :],
            kv_s(k_s), kv_s(v_s),
            u_s.reshape(Bs, Ts, POOL_WIDTH)[None, :, Ts - POOL_HIST:, :])
```

```python
import functools

import jax
import jax.numpy as jnp
from jax import lax
from jax.experimental import pallas as pl
from jax.experimental.pallas import tpu as pltpu

D_MODEL = 1024
CHUNK = 64
WINDOW = 128
N_HEADS = 8
HEAD_DIM = 64
N_KV_HEADS = 2
GQA_GROUP = N_HEADS // N_KV_HEADS
Q_DIM = N_HEADS * HEAD_DIM
KV_DIM = N_KV_HEADS * HEAD_DIM
POOL_WIDTH = 512
POOL_WINDOWS = (2, 4, 8, 16)
POOL_GROUP_DIM = POOL_WIDTH // len(POOL_WINDOWS)
POOL_HIST = max(POOL_WINDOWS) - 1
HIST_ROWS = 16
IN_PROJ_DIM = Q_DIM + 2 * KV_DIM + POOL_WIDTH
D_FF = 4 * D_MODEL
PLE_DIM = 256
PAST_LEN = 2048
RMS_EPS = 1e-6
LANES = 128
HALF = LANES // 2

SEQ_TILE = 512
FF_CHUNK = 512
DOWN_BLOCK = 256
SAMPLE_STEPS = 8
VMEM_LIMIT_BYTES = 58 * 1024 * 1024

_F32 = jnp.float32
_BF16 = jnp.bfloat16
_VEC_ROWS = {"ln_mix": (0, D_MODEL), "g_attn": (1, Q_DIM), "pool_scale": (2, POOL_WIDTH),
             "g_pool": (3, POOL_WIDTH), "ln_ffn": (4, D_MODEL), "ln_ple": (5, D_MODEL),
             "ln_final": (6, D_MODEL), "sink_row": (7, N_HEADS * 2 * CHUNK)}
_PAIR_HEAD_ORDER = tuple(h for p in range(GQA_GROUP) for h in (p, GQA_GROUP + p))


def _rms(x, g):
    ms = jnp.mean(x * x, axis=-1, keepdims=True)
    return x * lax.rsqrt(ms + RMS_EPS) * g


def _dot(a, w):
    return jnp.dot(a.astype(_BF16), w, preferred_element_type=_F32)


def _dot_nt(a, b):
    return lax.dot_general(a, b, (((1,), (1,)), ((), ())), preferred_element_type=_F32)


def _stack_heads(q_rows):
    lane = lax.broadcasted_iota(jnp.int32, (q_rows.shape[0], LANES), 1)
    lo, hi = [], []
    zero = jnp.zeros((q_rows.shape[0], LANES), q_rows.dtype)
    for p in range(GQA_GROUP):
        blk = q_rows[:, p * LANES:(p + 1) * LANES]
        lo.append(jnp.where(lane < HALF, blk, zero))
        hi.append(jnp.where(lane >= HALF, blk, zero))
    return jnp.concatenate(lo + hi, axis=0)


def _unstack_heads(o, rows):
    lane = lax.broadcasted_iota(jnp.int32, (rows, LANES), 1)
    blocks = []
    for p in range(GQA_GROUP):
        o_lo = o[p * rows:(p + 1) * rows]
        o_hi = o[(GQA_GROUP + p) * rows:(GQA_GROUP + p + 1) * rows]
        blocks.append(jnp.where(lane < HALF, o_lo, o_hi))
    return jnp.concatenate(blocks, axis=-1)


def _sink_softmax_pv(score_parts, value_parts, sink_col):
    m = sink_col
    for s in score_parts:
        m = jnp.maximum(m, jnp.max(s, axis=-1, keepdims=True))
    den = jnp.exp(sink_col - m)
    o = None
    for s, v in zip(score_parts, value_parts):
        e = jnp.exp(s - m)
        den = den + jnp.sum(e, axis=-1, keepdims=True)
        pv = jnp.dot(e.astype(_BF16), v, preferred_element_type=_F32)
        o = pv if o is None else o + pv
    return o * (1.0 / den)


def _attention_transposed(qT, kbuf, vTbuf, sink_row, first_tile, aT_ref, interleave=None):
    T = qT.shape[1]
    pair = 2 * CHUNK
    n_keys = WINDOW + pair
    n_cols = N_HEADS * pair
    zero = jnp.zeros((HEAD_DIM, pair), qT.dtype)
    lane = lax.broadcasted_iota(jnp.int32, (CHUNK, n_cols), 1) % pair
    neg = jnp.float32(-jnp.inf)
    for j in range(T // pair):
        cols = slice(j * pair, (j + 1) * pair)
        blocks = []
        for rb in range(N_HEADS):
            qh = qT[rb * HEAD_DIM:(rb + 1) * HEAD_DIM, cols]
            blocks.append(jnp.concatenate([qh, zero] if rb % 2 == 0 else [zero, qh], axis=0))
        rhs = jnp.concatenate(blocks, axis=1)
        sc = jnp.dot(kbuf[j * pair:j * pair + n_keys, :], rhs, preferred_element_type=_F32)
        s0 = jnp.where(lane >= CHUNK, neg, sc[0:CHUNK])
        s3 = jnp.where(lane < CHUNK, neg, sc[3 * CHUNK:4 * CHUNK])
        s1 = sc[CHUNK:2 * CHUNK]
        if j == 0:
            s0 = jnp.where(first_tile, neg, s0)
            s1 = jnp.where(first_tile, neg, s1)
        sc = jnp.concatenate([s0, s1, sc[2 * CHUNK:3 * CHUNK], s3], axis=0)
        m = jnp.maximum(jnp.max(sc, axis=0, keepdims=True), sink_row)
        if interleave is not None:
            interleave(2 * j)
        e = jnp.exp(sc - m)
        den = jnp.sum(e, axis=0, keepdims=True) + jnp.exp(sink_row - m)
        oT = jnp.dot(vTbuf[:, j * pair:j * pair + n_keys], e.astype(_BF16),
                     preferred_element_type=_F32)
        oT = oT * (1.0 / den)
        outs = []
        for rb in range(N_HEADS):
            g = rb % 2
            outs.append(oT[g * HEAD_DIM:(g + 1) * HEAD_DIM, rb * pair:(rb + 1) * pair])
        aT_ref[:, cols] = jnp.concatenate(outs, axis=0)
        if interleave is not None:
            interleave(2 * j + 1)


def _pool_windows(ext):
    outs = []
    for g, w in enumerate(POOL_WINDOWS):
        acc = ext[:, g * POOL_GROUP_DIM:(g + 1) * POOL_GROUP_DIM]
        span = 1
        while span < w:
            acc = acc + pltpu.roll(acc, span, 0)
            span *= 2
        outs.append(acc)
    return outs


def _pool_mix(sums, u, inv_cnt, w_pool_ref, pool_scale):
    outs = []
    for g in range(len(POOL_WINDOWS)):
        cols = slice(g * POOL_GROUP_DIM, (g + 1) * POOL_GROUP_DIM)
        d = sums[g] * inv_cnt[g] - u[:, cols]
        outs.append(_dot(d, w_pool_ref[g]))
    return jnp.concatenate(outs, axis=-1) * pool_scale


def _mix_out(x, a, b, g_attn, g_pool, w_out_ref, ln_ffn):
    mix = jnp.concatenate([_rms(a, g_attn), _rms(b, g_pool)], axis=-1)
    x = x + _dot(mix, w_out_ref[...])
    return x, _rms(x, ln_ffn).astype(_BF16)


def _vec(vec_ref, name):
    row, width = _VEC_ROWS[name]
    return vec_ref[row:row + 1, :width]


def _ffn_up(hf, c, w_up_ref, act_ref):
    cols = slice(c * FF_CHUNK, (c + 1) * FF_CHUNK)
    up = jnp.dot(hf, w_up_ref[:, cols], preferred_element_type=_F32)
    act_ref[:, cols] = jnp.square(jnp.maximum(up, 0.0)).astype(_BF16)


def _ffn_down(n, w_down_ref, act_ref):
    cols = slice(n * DOWN_BLOCK, (n + 1) * DOWN_BLOCK)
    return jnp.dot(act_ref[...], w_down_ref[:, cols], preferred_element_type=_F32)


def _prompt_kernel(x_ref, p_ref, x1s_ref, hfs_ref, ps_ref, vec_ref, w_in_ref, w_pool_ref, w_out_ref,
                   w_up_ref, w_down_ref, w_gate_ref, w_ple_ref,
                   y_ref, ys_ref, k_out_ref, v_out_ref, pool_out_ref,
                   kbuf, vTbuf, ubuf, aT_buf, x1_carry, hf_carry, act_buf, *, n_tiles, n_total):
    s = pl.program_id(0)
    i = lax.rem(jnp.minimum(s, n_total - 1), n_tiles)
    T = SEQ_TILE

    @pl.when(s == 0)
    def _():
        x1_carry[...] = x1s_ref[...]
        hf_carry[...] = hfs_ref[...]

    @pl.when(i == 0)
    def _():
        kbuf[0:WINDOW, :] = jnp.zeros((WINDOW, KV_DIM), _BF16)
        vTbuf[:, 0:WINDOW] = jnp.zeros((KV_DIM, WINDOW), _BF16)
        ubuf[0:HIST_ROWS, :] = jnp.zeros((HIST_ROWS, POOL_WIDTH), _F32)

    @pl.when(s < n_total)
    def _both_halves():
        n_up = D_FF // FF_CHUNK
        n_down = D_MODEL // DOWN_BLOCK
        n_slots = 2 * (T // (2 * CHUNK))
        assert n_up == n_slots and n_down * 2 == n_slots
        ffn_out = []

        def ffn_piece(slot):
            if slot < n_slots // 2:
                _ffn_up(hf_carry[...], 2 * slot, w_up_ref, act_buf)
                _ffn_up(hf_carry[...], 2 * slot + 1, w_up_ref, act_buf)
            else:
                ffn_out.append(_ffn_down(slot - n_slots // 2, w_down_ref, act_buf))

        x = x_ref[...]
        z = _dot(_rms(x, _vec(vec_ref, "ln_mix")), w_in_ref[...])
        q = z[:, :Q_DIM] * (HEAD_DIM ** -0.5)
        k = z[:, Q_DIM:Q_DIM + KV_DIM]
        v = z[:, Q_DIM + KV_DIM:Q_DIM + 2 * KV_DIM]
        u = z[:, Q_DIM + 2 * KV_DIM:]
        kbuf[WINDOW:WINDOW + T, :] = k.astype(_BF16)
        vTbuf[:, WINDOW:WINDOW + T] = v.T.astype(_BF16)
        ubuf[HIST_ROWS:HIST_ROWS + T, :] = u

        _attention_transposed(q.T.astype(_BF16), kbuf, vTbuf, _vec(vec_ref, "sink_row"), i == 0, aT_buf,
                              interleave=ffn_piece)
        a = aT_buf[...].T

        sums = [sg[HIST_ROWS:] for sg in _pool_windows(ubuf[...])]
        pos1 = (i * T + 1 + lax.broadcasted_iota(jnp.int32, (T, 1), 0))
        inv_cnt = [1.0 / jnp.minimum(pos1, w).astype(_F32) for w in POOL_WINDOWS]
        b = _pool_mix(sums, u, inv_cnt, w_pool_ref, _vec(vec_ref, "pool_scale"))
        mix = jnp.concatenate([_rms(a, _vec(vec_ref, "g_attn")), _rms(b, _vec(vec_ref, "g_pool"))],
                              axis=-1)

        x2 = x1_carry[...] + jnp.concatenate(ffn_out, axis=-1)
        n2 = _rms(x2, _vec(vec_ref, "ln_ple"))
        x1 = x + _dot(mix, w_out_ref[...])
        gate_pre = _dot(n2, w_gate_ref[...])
        ple = _dot(jnp.where(s == 0, ps_ref[...], p_ref[...]), w_ple_ref[...])
        x1_carry[...] = x1
        hf_carry[...] = _rms(x1, _vec(vec_ref, "ln_ffn")).astype(_BF16)
        y_ref[...] = _rms(x2 + jax.nn.sigmoid(gate_pre) * ple, _vec(vec_ref, "ln_final"))

        @pl.when(s == 0)
        def _():
            ys_ref[...] = y_ref[...]

        kbuf[0:WINDOW, :] = kbuf[T:T + WINDOW, :]
        vTbuf[:, 0:WINDOW] = vTbuf[:, T:T + WINDOW]
        ubuf[0:HIST_ROWS, :] = ubuf[T:T + HIST_ROWS, :]

        @pl.when(i == n_tiles - 1)
        def _():
            k_out_ref[...] = k[T - WINDOW:]
            v_out_ref[...] = v[T - WINDOW:]
            pool_out_ref[...] = u[T - HIST_ROWS:]

    @pl.when(s == n_total)
    def _last_mlp_half():
        for c in range(D_FF // FF_CHUNK):
            _ffn_up(hf_carry[...], c, w_up_ref, act_buf)
        x2 = x1_carry[...] + jnp.concatenate(
            [_ffn_down(n, w_down_ref, act_buf) for n in range(D_MODEL // DOWN_BLOCK)], axis=-1)
        gate_pre = _dot(_rms(x2, _vec(vec_ref, "ln_ple")), w_gate_ref[...])
        ple = _dot(p_ref[...], w_ple_ref[...])
        y_ref[...] = _rms(x2 + jax.nn.sigmoid(gate_pre) * ple, _vec(vec_ref, "ln_final"))


def _sample_kernel(x_ref, ck_ref, cv_ref, hist_ref, sink_ref, vec_ref, w_in_ref, w_pool_ref,
                   w_out_ref, w_up_ref, w_down_ref, w_gate_ref, w_ple_ref,
                   x1_ref, hf_ref, k_out_ref, v_out_ref, u_out_ref,
                   w_up_bf_ref, w_down_bf_ref, w_gate_bf_ref, w_ple_bf_ref,
                   qbuf, kbuf, vbuf, abuf, *, n_streams, n_frames, n_steps):
    g = pl.program_id(0)

    w_up_bf_ref[...] = w_up_ref[...].astype(_BF16)
    w_down_bf_ref[...] = w_down_ref[...].astype(_BF16)
    w_gate_bf_ref[...] = w_gate_ref[...].astype(_BF16)
    w_ple_bf_ref[...] = w_ple_ref[...].astype(_BF16)

    @pl.when(g == 0)
    def _():
        z = _dot(_rms(x_ref[...], _vec(vec_ref, "ln_mix")), w_in_ref[...])
        k = z[:, Q_DIM:Q_DIM + KV_DIM]
        v = z[:, Q_DIM + KV_DIM:Q_DIM + 2 * KV_DIM]
        qbuf[...] = (z[:, :Q_DIM] * (HEAD_DIM ** -0.5)).astype(_BF16)
        kbuf[...] = k.astype(_BF16)
        vbuf[...] = v.astype(_BF16)
        k_out_ref[...] = k
        v_out_ref[...] = v
        u_out_ref[...] = z[:, Q_DIM + 2 * KV_DIM:]

    sink_col = sink_ref[...]
    streams_per_step = n_streams // n_steps
    for t in range(streams_per_step):
        s_idx = g * streams_per_step + t
        rows = pl.ds(pl.multiple_of(s_idx * n_frames, n_frames), n_frames)
        qs = _stack_heads(qbuf[rows, :])
        s_cache = _dot_nt(qs, ck_ref[s_idx].astype(_BF16))
        s_new = _dot_nt(qs, kbuf[rows, :])
        o = _sink_softmax_pv([s_cache, s_new], [cv_ref[s_idx].astype(_BF16), vbuf[rows, :]],
                             sink_col)
        abuf[rows, :] = _unstack_heads(o, n_frames)

    @pl.when(g == n_steps - 1)
    def _():
        u = u_out_ref[...]
        ext = jnp.concatenate([hist_ref[...], u.reshape(n_streams, n_frames, POOL_WIDTH)], axis=1)
        ext = ext.reshape(n_streams * (HIST_ROWS + n_frames), POOL_WIDTH)
        sums = [sg.reshape(n_streams, HIST_ROWS + n_frames, POOL_GROUP_DIM)[:, HIST_ROWS:, :]
                .reshape(n_streams * n_frames, POOL_GROUP_DIM) for sg in _pool_windows(ext)]
        inv_cnt = [1.0 / min(PAST_LEN + 1, w) for w in POOL_WINDOWS]
        b = _pool_mix(sums, u, inv_cnt, w_pool_ref, _vec(vec_ref, "pool_scale"))

        x1, hf = _mix_out(x_ref[...], abuf[...], b, _vec(vec_ref, "g_attn"), _vec(vec_ref, "g_pool"),
                          w_out_ref, _vec(vec_ref, "ln_ffn"))
        x1_ref[...] = x1
        hf_ref[...] = hf


def _resident(shape):
    zeros = (0,) * len(shape)
    return pl.BlockSpec(shape, lambda *_: zeros, pipeline_mode=pl.Buffered(1))


def _pair_order(t, axis):
    shape = t.shape
    t = t.reshape(shape[:axis] + (N_KV_HEADS, GQA_GROUP, HEAD_DIM) + shape[axis + 1:])
    return jnp.swapaxes(t, axis, axis + 1).reshape(shape)


def _layer_weights(ln_mix, w_in, attn_sinks, g_attn_out, w_pool, pool_scale, g_pool_out, w_out,
                   ln_ffn, w_up, w_down, ln_ple, w_ple_gate, w_ple_proj, ln_final):
    w_in_p = jnp.concatenate([_pair_order(w_in[:, :Q_DIM], 1), w_in[:, Q_DIM:]],
                             axis=1).astype(_BF16)
    w_out_p = jnp.concatenate([_pair_order(w_out[:Q_DIM], 0), w_out[Q_DIM:]],
                              axis=0).astype(_BF16)
    vec_values = {"ln_mix": ln_mix, "g_attn": _pair_order(g_attn_out, 0), "pool_scale": pool_scale,
                  "g_pool": g_pool_out, "ln_ffn": ln_ffn, "ln_ple": ln_ple, "ln_final": ln_final,
                  "sink_row": jnp.repeat(attn_sinks[jnp.array(_PAIR_HEAD_ORDER)], 2 * CHUNK)}
    rows = [None] * len(_VEC_ROWS)
    for name, (r, width) in _VEC_ROWS.items():
        rows[r] = jnp.pad(vec_values[name].astype(_F32), (0, D_MODEL - width))
    return dict(
        sinks=attn_sinks.astype(_F32), vecs=jnp.stack(rows), w_in=w_in_p,
        w_pool=w_pool.astype(_BF16), w_out=w_out_p, w_up=w_up, w_down=w_down, w_gate=w_ple_gate,
        w_ple=w_ple_proj)


_MIXER_WEIGHTS = ("vecs", "w_in", "w_pool", "w_out")
_MLP_WEIGHTS = ("w_up", "w_down", "w_gate", "w_ple")


def _prompt_layer(x, p, x1_s, hf_s, p_s, lw, mlp_weights):
    B, S, _ = x.shape
    T = SEQ_TILE
    assert S % T == 0 and T % (2 * CHUNK) == 0 and T >= WINDOW
    assert x1_s.shape == (T, D_MODEL), "the sample rows must fill exactly one tile"
    n_tiles = S // T
    n_total = B * n_tiles
    weights = [lw[n] for n in _MIXER_WEIGHTS] + list(mlp_weights)
    sample = [x1_s, hf_s, p_s]

    def mixer_tile(s):
        return jnp.minimum(s, n_total - 1)

    def mlp_tile(s):
        return jnp.maximum(s - 1, 0)

    def rows(width, which):
        return pl.BlockSpec((None, T, width),
                            lambda s: (which(s) // n_tiles, which(s) % n_tiles, 0))

    last = lambda nrows, width: pl.BlockSpec((None, nrows, width),
                                             lambda s: (mixer_tile(s) // n_tiles, 0, 0))
    return pl.pallas_call(
        functools.partial(_prompt_kernel, n_tiles=n_tiles, n_total=n_total),
        grid=(n_total + 1,),
        in_specs=[rows(D_MODEL, mixer_tile), rows(PLE_DIM, mlp_tile)]
                 + [_resident(a.shape) for a in sample + weights],
        out_specs=[rows(D_MODEL, mlp_tile), pl.BlockSpec((T, D_MODEL), lambda s: (0, 0)),
                   last(WINDOW, KV_DIM), last(WINDOW, KV_DIM), last(HIST_ROWS, POOL_WIDTH)],
        out_shape=[jax.ShapeDtypeStruct((B, S, D_MODEL), _F32),
                   jax.ShapeDtypeStruct((T, D_MODEL), _F32),
                   jax.ShapeDtypeStruct((B, WINDOW, KV_DIM), _F32),
                   jax.ShapeDtypeStruct((B, WINDOW, KV_DIM), _F32),
                   jax.ShapeDtypeStruct((B, HIST_ROWS, POOL_WIDTH), _F32)],
        scratch_shapes=[pltpu.VMEM((WINDOW + T, KV_DIM), _BF16),
                        pltpu.VMEM((KV_DIM, WINDOW + T), _BF16),
                        pltpu.VMEM((HIST_ROWS + T, POOL_WIDTH), _F32),
                        pltpu.VMEM((Q_DIM, T), _F32),
                        pltpu.VMEM((T, D_MODEL), _F32),
                        pltpu.VMEM((T, D_MODEL), _BF16),
                        pltpu.VMEM((T, D_FF), _BF16)],
        compiler_params=pltpu.CompilerParams(
            dimension_semantics=("arbitrary",), vmem_limit_bytes=VMEM_LIMIT_BYTES),
        name="prompt_layer",
    )(x, p, *sample, *weights)


def _sample_layer(x, cache_k, cache_v, state_pool, lw):
    Bs, Ts, _ = x.shape
    R = Bs * Ts
    n_steps = SAMPLE_STEPS
    assert Ts == HIST_ROWS, "new frames per stream must fill one 16-row block"
    assert Bs % n_steps == 0
    sink_col = jnp.repeat(lw["sinks"], Ts).reshape(N_HEADS * Ts, 1)
    hist = jnp.pad(state_pool, ((0, 0), (HIST_ROWS - POOL_HIST, 0), (0, 0)))
    resident = [x.reshape(R, D_MODEL), cache_k.reshape(Bs, WINDOW, KV_DIM),
                cache_v.reshape(Bs, WINDOW, KV_DIM), hist, sink_col] + [lw[n] for n in _MIXER_WEIGHTS]
    mlp_weights = [lw[n] for n in _MLP_WEIGHTS]
    row_block = lambda w: pl.BlockSpec((w.shape[0] // n_steps, w.shape[1]), lambda g: (g, 0))
    outs = [((R, D_MODEL), _F32), ((R, D_MODEL), _BF16), ((R, KV_DIM), _F32), ((R, KV_DIM), _F32),
            ((R, POOL_WIDTH), _F32)]
    res = pl.pallas_call(
        functools.partial(_sample_kernel, n_streams=Bs, n_frames=Ts, n_steps=n_steps),
        grid=(n_steps,),
        in_specs=[_resident(a.shape) for a in resident] + [row_block(w) for w in mlp_weights],
        out_specs=[pl.BlockSpec(shape, lambda g: (0, 0)) for shape, _ in outs]
                  + [row_block(w) for w in mlp_weights],
        out_shape=[jax.ShapeDtypeStruct(shape, dtype) for shape, dtype in outs]
                  + [jax.ShapeDtypeStruct(w.shape, _BF16) for w in mlp_weights],
        scratch_shapes=[pltpu.VMEM((R, Q_DIM), _BF16), pltpu.VMEM((R, KV_DIM), _BF16),
                        pltpu.VMEM((R, KV_DIM), _BF16), pltpu.VMEM((R, Q_DIM), _F32)],
        compiler_params=pltpu.CompilerParams(
            dimension_semantics=("arbitrary",), vmem_limit_bytes=VMEM_LIMIT_BYTES),
        name="sample_mixer",
    )(*resident, *mlp_weights)
    return res[:len(outs)], res[len(outs):]


def kernel(x_prompt, x_sample, cache_k, cache_v, state_pool, p_prompt, p_sample, ln_mix, w_in,
           attn_sinks, g_attn_out, w_pool, pool_scale, g_pool_out, w_out, ln_ffn, w_up, w_down,
           ln_ple, w_ple_gate, w_ple_proj, ln_final):
    depth = ln_mix.shape[0]
    assert depth == 1, "the final norm is fused into the single layer's kernel"
    B = x_prompt.shape[0]
    Bs, Ts, _ = x_sample.shape
    lw = _layer_weights(ln_mix[0], w_in[0], attn_sinks[0], g_attn_out[0], w_pool[0], pool_scale[0],
                        g_pool_out[0], w_out[0], ln_ffn[0], w_up[0], w_down[0], ln_ple[0],
                        w_ple_gate[0], w_ple_proj[0], ln_final)
    (x1_s, hf_s, k_s, v_s, u_s), mlp_weights = _sample_layer(x_sample, cache_k[0], cache_v[0],
                                                             state_pool[0], lw)
    y_p, y_s, k_p, v_p, pool_p = _prompt_layer(x_prompt, p_prompt[0], x1_s, hf_s,
                                               p_sample[0].reshape(Bs * Ts, PLE_DIM), lw, mlp_weights)
    kv_p = lambda t: t.reshape(1, B, WINDOW, N_KV_HEADS, HEAD_DIM)
    kv_s = lambda t: t.reshape(1, Bs, Ts, N_KV_HEADS, HEAD_DIM)
    return (y_p, y_s.reshape(Bs, Ts, D_MODEL), kv_p(k_p), kv_p(v_p),
            pool_p[None, :, HIST_ROWS - POOL_HIST:, :],
            kv_s(k_s), kv_s(v_s),
            u_s.reshape(Bs, Ts, POOL_WIDTH)[None, :, Ts - POOL_HIST:, :])
```
